```python
import math, functools
import jax
import jax.numpy as jnp
from jax import lax
import numpy as np

D_MODEL = 1024
BATCH = 8
SEQ = 2048
DEPTH = 4
DEC_BATCH = 128
DEC_SEQ = 4
PAST_LEN = 8192
PAGE_SIZE = 128

N_EVEN = (DEPTH + 1) // 2
N_ODD = DEPTH // 2
MIX_W = D_MODEL
A_GROUPS = 4
A_W = MIX_W // 2
A_GW = A_W // A_GROUPS
CHUNK = 128
MLA_HEADS = 4
MLA_NOPE = 128
MLA_ROPE = 64
MLA_VDIM = 128
MLA_Q_RANK = 384
MLA_KV_RANK = 256
MLA_THETA = 10000.0
B_W = MLA_HEADS * MLA_VDIM
EVEN_IN = 2 * A_W + MLA_Q_RANK + MLA_KV_RANK + MLA_ROPE
C_W = MIX_W // 2
CONV_W = 31
DIFF_HEADS = 4
DIFF_DH = 64
DIFF_VDIM = 2 * DIFF_DH
DIFF_ROT = DIFF_DH // 4
ROPE_THETA = 500000.0
D_W = DIFF_HEADS * DIFF_VDIM
DIFF_QK = DIFF_HEADS * 2 * DIFF_DH
ODD_IN = 2 * C_W + 2 * DIFF_QK + D_W
D_FF = 2816
FFN_CONV = 3
Q_BLOCK = 128
EPS = 1e-6
NEG_INF = -1e30

kernel_name = 'hybrid_gmlp_mla_conformer_diffattn_step'


def rmsnorm(x, g):
    xf = x.astype(jnp.float32)
    y = xf * lax.rsqrt(jnp.mean(xf * xf, axis=-1, keepdims=True) + EPS)
    return (y * g.astype(jnp.float32)).astype(x.dtype)


def layernorm(x, g, b):
    xf = x.astype(jnp.float32)
    mu = jnp.mean(xf, axis=-1, keepdims=True)
    d = xf - mu
    y = d * lax.rsqrt(jnp.mean(d * d, axis=-1, keepdims=True) + EPS)
    return (y * g.astype(jnp.float32) + b.astype(jnp.float32)).astype(x.dtype)


def rope(x, pos, theta, rot):
    half = rot // 2
    inv = jnp.power(theta, -(jnp.arange(half, dtype=jnp.float32) * 2.0 / rot))
    ang = pos.astype(jnp.float32)[:, None] * inv[None, :]
    shape = (ang.shape[0],) + (1,) * (x.ndim - 3) + (half,)
    cos = jnp.cos(ang).reshape(shape).astype(x.dtype)
    sin = jnp.sin(ang).reshape(shape).astype(x.dtype)
    x1, x2, rest = x[..., :half], x[..., half:rot], x[..., rot:]
    return jnp.concatenate([x1 * cos - x2 * sin, x2 * cos + x1 * sin, rest], axis=-1)


def causal_dwconv(xpad, w, b):
    c = xpad.shape[-1]
    y = lax.conv_general_dilated(xpad, w[:, None, :].astype(xpad.dtype), window_strides=(1,),
                                 padding='VALID', dimension_numbers=('NWC', 'WIO', 'NWC'),
                                 feature_group_count=c)
    return y + b.astype(xpad.dtype)


def chunk_spatial_gate(u, v, w_s, b_s):
    bsz, t, g, c = v.shape
    L = t if t <= CHUNK else CHUNK
    pad = (-t) % L
    if pad:
        v = jnp.pad(v, ((0, 0), (0, pad), (0, 0), (0, 0)))
    n = (t + pad) // L
    mask = jnp.arange(L)[:, None] >= jnp.arange(L)[None, :]
    w = jnp.where(mask[None], w_s[:, :L, :L], 0.0).astype(v.dtype)
    mix = jnp.einsum('gts,bnsgc->bntgc', w, v.reshape(bsz, n, L, g, c))
    mix = mix + b_s[:, :L].T[:, :, None].astype(v.dtype)
    return u * mix.reshape(bsz, n * L, g, c)[:, :t]


def causal_probs(q, k, q_pos, k_pos):
    s = jnp.einsum('...qhd,...khd->...hqk', q, k, preferred_element_type=jnp.float32)
    s = s * (q.shape[-1] ** -0.5)
    s = jnp.where(k_pos[None, :] <= q_pos[:, None], s, NEG_INF)
    return jax.nn.softmax(s, axis=-1)


def mla_attend(q, k, v, q_pos, k_pos):
    p = causal_probs(q, k, q_pos, k_pos)
    return jnp.einsum('...hqk,...khd->...qhd', p.astype(v.dtype), v)


def diff_attend(q, k, v, q_pos, k_pos, lam):
    p1 = causal_probs(q[..., 0, :], k[..., 0, :], q_pos, k_pos)
    p2 = causal_probs(q[..., 1, :], k[..., 1, :], q_pos, k_pos)
    return jnp.einsum('...hqk,...khd->...qhd', (p1 - lam * p2).astype(v.dtype), v)


def prompt_attention(attend, q, k, v):
    bsz, s = q.shape[:2]
    nb = s // Q_BLOCK
    qb = jnp.moveaxis(q.reshape((bsz, nb, Q_BLOCK) + q.shape[2:]), 1, 0)
    k_pos = jnp.arange(s, dtype=jnp.int32)

    def one(args):
        i, qi = args
        q_pos = i * Q_BLOCK + jnp.arange(Q_BLOCK, dtype=jnp.int32)
        return attend(qi, k, v, q_pos, k_pos)

    out = lax.map(one, (jnp.arange(nb, dtype=jnp.int32), qb))
    return jnp.moveaxis(out, 0, 1).reshape((bsz, s) + out.shape[3:])


def sample_attention(attend, q, k_new, v_new, gather, page_table):
    s = q.shape[1]

    def one(args):
        pt, qb, kb, vb = args
        k_past, v_past = gather(pt)
        t = k_past.shape[0]
        k = jnp.concatenate([k_past, kb], axis=0)
        v = jnp.concatenate([v_past, vb], axis=0)
        q_pos = t + jnp.arange(s, dtype=jnp.int32)
        k_pos = jnp.arange(t + s, dtype=jnp.int32)
        return attend(qb, k, v, q_pos, k_pos)

    return lax.map(one, (page_table, q, k_new, v_new))


def mla_keys(c, kpe, w_ukv, g_kn):
    kv = jnp.einsum('...r,rhe->...he', c, w_ukv)
    k_nope, v = kv[..., :MLA_NOPE], kv[..., MLA_NOPE:]
    kr = jnp.broadcast_to(kpe[..., None, :], k_nope.shape[:-1] + (MLA_ROPE,))
    k = rmsnorm(jnp.concatenate([k_nope, kr], axis=-1), g_kn)
    return k, v


def even_mixer(h, pos, w_in, w_out, g_v, w_s, b_s, g_cq, w_uq, g_ckv, w_ukv, g_qn, g_kn,
               cache_lat=None, cache_rope=None, page_table=None):
    bsz, t = h.shape[:2]
    z = h @ w_in
    u, v, cq, ckv, kpe = jnp.split(
        z, [A_W, 2 * A_W, 2 * A_W + MLA_Q_RANK, 2 * A_W + MLA_Q_RANK + MLA_KV_RANK], axis=-1)
    u = jax.nn.gelu(u).reshape(bsz, t, A_GROUPS, A_GW)
    v = rmsnorm(jax.nn.gelu(v).reshape(bsz, t, A_GROUPS, A_GW), g_v.reshape(A_GROUPS, A_GW))
    o_a = chunk_spatial_gate(u, v, w_s, b_s).reshape(bsz, t, A_W)
    q = jnp.einsum('btr,rhe->bthe', rmsnorm(cq, g_cq), w_uq)
    q = jnp.concatenate([q[..., :MLA_NOPE], rope(q[..., MLA_NOPE:], pos, MLA_THETA, MLA_ROPE)], axis=-1)
    q = rmsnorm(q, g_qn)
    c = rmsnorm(ckv, g_ckv)
    kpe = rope(kpe[:, :, None, :], pos, MLA_THETA, MLA_ROPE)[:, :, 0]
    k, vv = mla_keys(c, kpe, w_ukv, g_kn)
    if page_table is None:
        o_b = prompt_attention(mla_attend, q, k, vv)
    else:
        def gather(pt):
            return mla_keys(cache_lat[pt].reshape(-1, MLA_KV_RANK),
                            cache_rope[pt].reshape(-1, MLA_ROPE), w_ukv, g_kn)
        o_b = sample_attention(mla_attend, q, k, vv, gather, page_table)
    o = jnp.concatenate([o_a, o_b.reshape(bsz, t, B_W)], axis=-1) @ w_out
    return o, v.reshape(bsz, t, A_W), c, kpe


def odd_mixer(h, pos, layer, w_in, w_out, conv_w, conv_b, ln_g, ln_b, g_q, g_k,
              lq1, lk1, lq2, lk2, g_sub, conv_state=None, cache_k=None, cache_v=None, page_table=None):
    bsz, t = h.shape[:2]
    z = h @ w_in
    a, gt, q, k, v = jnp.split(z, [C_W, 2 * C_W, 2 * C_W + DIFF_QK, 2 * C_W + 2 * DIFF_QK], axis=-1)
    xc = a * jax.nn.sigmoid(gt)
    prev = jnp.zeros((bsz, CONV_W - 1, C_W), xc.dtype) if conv_state is None else conv_state.astype(xc.dtype)
    xpad = jnp.concatenate([prev, xc], axis=1)
    o_c = jax.nn.silu(layernorm(causal_dwconv(xpad, conv_w, conv_b), ln_g, ln_b))
    new_conv = xpad[:, -(CONV_W - 1):]
    q = rope(rmsnorm(q.reshape(bsz, t, DIFF_HEADS, 2, DIFF_DH), g_q), pos, ROPE_THETA, DIFF_ROT)
    k = rope(rmsnorm(k.reshape(bsz, t, DIFF_HEADS, 2, DIFF_DH), g_k), pos, ROPE_THETA, DIFF_ROT)
    v = v.reshape(bsz, t, DIFF_HEADS, DIFF_VDIM)
    lam_init = 0.8 - 0.6 * math.exp(-0.3 * layer)
    f32 = jnp.float32
    lam = (jnp.exp(jnp.sum(lq1.astype(f32) * lk1.astype(f32)))
           - jnp.exp(jnp.sum(lq2.astype(f32) * lk2.astype(f32))) + lam_init)
    attend = functools.partial(diff_attend, lam=lam)
    if page_table is None:
        o_d = prompt_attention(attend, q, k, v)
    else:
        def gather(pt):
            return (cache_k[pt].reshape((-1,) + cache_k.shape[2:]),
                    cache_v[pt].reshape((-1,) + cache_v.shape[2:]))
        o_d = sample_attention(attend, q, k, v, gather, page_table)
    o_d = rmsnorm(o_d, g_sub) * (1.0 - lam_init)
    o = jnp.concatenate([o_c, o_d.reshape(bsz, t, D_W)], axis=-1) @ w_out
    return o, new_conv, k, v


def conv_ffn(h, w_up, conv_w, conv_b, w_down, state=None):
    bsz = h.shape[0]
    g, u = jnp.split(h @ w_up, [D_FF], axis=-1)
    prev = jnp.zeros((bsz, FFN_CONV - 1, D_FF), g.dtype) if state is None else state.astype(g.dtype)
    gpad = jnp.concatenate([prev, g], axis=1)
    y = (jax.nn.silu(causal_dwconv(gpad, conv_w, conv_b)) * u) @ w_down
    return y, gpad[:, -(FFN_CONV - 1):]


def setup_inputs(seed: int = 0) -> dict:
    key = jax.random.key(seed)
    ks = iter(jax.random.split(key, 48))
    f32 = jnp.float32

    def nrm(shape, scale=1.0):
        return jax.random.normal(next(ks), shape, f32) * scale

    def gain(shape):
        return 1.0 + nrm(shape, 0.02)

    n_pages = PAST_LEN // PAGE_SIZE
    n_used = DEC_BATCH * n_pages
    n_pool = n_used + n_used // 4
    page_table = jax.random.permutation(next(ks), n_pool)[:n_used].reshape(DEC_BATCH, n_pages).astype(jnp.int32)
    return {
        'x_prompt': nrm((BATCH, SEQ, D_MODEL)),
        'x_sample': nrm((DEC_BATCH, DEC_SEQ, D_MODEL)),
        'cache_mla_latent': nrm((N_EVEN, n_pool, PAGE_SIZE, MLA_KV_RANK)),
        'cache_mla_rope': nrm((N_EVEN, n_pool, PAGE_SIZE, MLA_ROPE)),
        'cache_diff_k': nrm((N_ODD, n_pool, PAGE_SIZE, DIFF_HEADS, 2, DIFF_DH)),
        'cache_diff_v': nrm((N_ODD, n_pool, PAGE_SIZE, DIFF_HEADS, DIFF_VDIM)),
        'state_conv': nrm((N_ODD, DEC_BATCH, CONV_W - 1, C_W), 0.5),
        'state_ffn': nrm((DEPTH, DEC_BATCH, FFN_CONV - 1, D_FF)),
        'page_table': page_table,
        'g_mix': gain((DEPTH, D_MODEL)),
        'g_ffn': gain((DEPTH, D_MODEL)),
        'w_in_even': nrm((N_EVEN, D_MODEL, EVEN_IN), D_MODEL ** -0.5),
        'w_out_even': nrm((N_EVEN, MIX_W, D_MODEL), MIX_W ** -0.5),
        'gmlp_g_v': gain((N_EVEN, A_W)),
        'gmlp_w_s': nrm((N_EVEN, A_GROUPS, CHUNK, CHUNK), CHUNK ** -0.5),
        'gmlp_b_s': gain((N_EVEN, A_GROUPS, CHUNK)),
        'mla_g_cq': gain((N_EVEN, MLA_Q_RANK)),
        'mla_w_uq': nrm((N_EVEN, MLA_Q_RANK, MLA_HEADS, MLA_NOPE + MLA_ROPE), MLA_Q_RANK ** -0.5),
        'mla_g_ckv': gain((N_EVEN, MLA_KV_RANK)),
        'mla_w_ukv': nrm((N_EVEN, MLA_KV_RANK, MLA_HEADS, MLA_NOPE + MLA_VDIM), MLA_KV_RANK ** -0.5),
        'mla_g_qn': gain((N_EVEN, MLA_NOPE + MLA_ROPE)),
        'mla_g_kn': gain((N_EVEN, MLA_NOPE + MLA_ROPE)),
        'w_in_odd': nrm((N_ODD, D_MODEL, ODD_IN), D_MODEL ** -0.5),
        'w_out_odd': nrm((N_ODD, MIX_W, D_MODEL), MIX_W ** -0.5),
        'conv_w': nrm((N_ODD, CONV_W, C_W), CONV_W ** -0.5),
        'conv_b': nrm((N_ODD, C_W), 0.02),
        'conv_ln_g': gain((N_ODD, C_W)),
        'conv_ln_b': nrm((N_ODD, C_W), 0.02),
        'diff_g_q': gain((N_ODD, DIFF_DH)),
        'diff_g_k': gain((N_ODD, DIFF_DH)),
        'diff_lq1': nrm((N_ODD, DIFF_DH), 0.1),
        'diff_lk1': nrm((N_ODD, DIFF_DH), 0.1),
        'diff_lq2': nrm((N_ODD, DIFF_DH), 0.1),
        'diff_lk2': nrm((N_ODD, DIFF_DH), 0.1),
        'diff_g_sub': gain((N_ODD, DIFF_VDIM)),
        'ffn_w_up': nrm((DEPTH, D_MODEL, 2 * D_FF), D_MODEL ** -0.5),
        'ffn_conv_w': nrm((DEPTH, FFN_CONV, D_FF), FFN_CONV ** -0.5),
        'ffn_conv_b': nrm((DEPTH, D_FF), 0.02),
        'ffn_w_down': nrm((DEPTH, D_FF, D_MODEL), D_FF ** -0.5),
    }


def reference(x_prompt, x_sample, cache_mla_latent, cache_mla_rope, cache_diff_k, cache_diff_v,
              state_conv, state_ffn, page_table, g_mix, g_ffn, w_in_even, w_out_even, gmlp_g_v,
              gmlp_w_s, gmlp_b_s, mla_g_cq, mla_w_uq, mla_g_ckv, mla_w_ukv, mla_g_qn, mla_g_kn,
              w_in_odd, w_out_odd, conv_w, conv_b, conv_ln_g, conv_ln_b, diff_g_q, diff_g_k,
              diff_lq1, diff_lk1, diff_lq2, diff_lk2, diff_g_sub, ffn_w_up, ffn_conv_w, ffn_conv_b,
              ffn_w_down):
    past_len = page_table.shape[1] * cache_mla_latent.shape[2]
    pos_p = jnp.arange(x_prompt.shape[1], dtype=jnp.int32)
    pos_s = past_len + jnp.arange(x_sample.shape[1], dtype=jnp.int32)
    xp, xs = x_prompt, x_sample
    gmlp_v_s, lat_p, rope_p, lat_s, rope_s = [], [], [], [], []
    conv_p, conv_s, dk_p, dv_p, dk_s, dv_s = [], [], [], [], [], []
    ffn_p, ffn_s = [], []
    for l in range(DEPTH):
        if l % 2 == 0:
            e = l // 2
            wts = (w_in_even[e], w_out_even[e], gmlp_g_v[e], gmlp_w_s[e], gmlp_b_s[e], mla_g_cq[e],
                   mla_w_uq[e], mla_g_ckv[e], mla_w_ukv[e], mla_g_qn[e], mla_g_kn[e])
            o_p, _, c_p, r_p = even_mixer(rmsnorm(xp, g_mix[l]), pos_p, *wts)
            o_s, v_s, c_s, r_s = even_mixer(rmsnorm(xs, g_mix[l]), pos_s, *wts,
                                            cache_lat=cache_mla_latent[e], cache_rope=cache_mla_rope[e],
                                            page_table=page_table)
            gmlp_v_s.append(v_s)
            lat_p.append(c_p)
            rope_p.append(r_p)
            lat_s.append(c_s)
            rope_s.append(r_s)
        else:
            o = l // 2
            wts = (w_in_odd[o], w_out_odd[o], conv_w[o], conv_b[o], conv_ln_g[o], conv_ln_b[o],
                   diff_g_q[o], diff_g_k[o], diff_lq1[o], diff_lk1[o], diff_lq2[o], diff_lk2[o],
                   diff_g_sub[o])
            o_p, cv_p, k_p, v_p = odd_mixer(rmsnorm(xp, g_mix[l]), pos_p, l, *wts)
            o_s, cv_s, k_s, v_s = odd_mixer(rmsnorm(xs, g_mix[l]), pos_s, l, *wts,
                                            conv_state=state_conv[o], cache_k=cache_diff_k[o],
                                            cache_v=cache_diff_v[o], page_table=page_table)
            conv_p.append(cv_p)
            conv_s.append(cv_s)
            dk_p.append(k_p)
            dv_p.append(v_p)
            dk_s.append(k_s)
            dv_s.append(v_s)
        xp = xp + o_p
        xs = xs + o_s
        f_p, st_p = conv_ffn(rmsnorm(xp, g_ffn[l]), ffn_w_up[l], ffn_conv_w[l], ffn_conv_b[l], ffn_w_down[l])
        f_s, st_s = conv_ffn(rmsnorm(xs, g_ffn[l]), ffn_w_up[l], ffn_conv_w[l], ffn_conv_b[l], ffn_w_down[l],
                             state=state_ffn[l])
        xp = xp + f_p
        xs = xs + f_s
        ffn_p.append(st_p)
        ffn_s.append(st_s)
    return (xp, xs, jnp.stack(gmlp_v_s), jnp.stack(lat_p), jnp.stack(rope_p), jnp.stack(lat_s),
            jnp.stack(rope_s), jnp.stack(conv_p), jnp.stack(conv_s), jnp.stack(dk_p), jnp.stack(dv_p),
            jnp.stack(dk_s), jnp.stack(dv_s), jnp.stack(ffn_p), jnp.stack(ffn_s))
```

```python
import functools
import math

import jax
import jax.numpy as jnp
from jax import lax
from jax.experimental import pallas as pl
from jax.experimental.pallas import tpu as pltpu

F32 = jnp.float32
BF16 = jnp.bfloat16

EPS = 1e-6
NEG_INF = -1e30
MLA_THETA = 10000.0
DIFF_THETA = 500000.0
GMLP_CHUNK = 128
LANES = 128
SUBLANES = 8
VMEM_LIMIT_BYTES = 56 * 1024 * 1024
TOKEN_TILE = 512
ATTN_TILE = 256
FFN_CHUNK = 256
PAGES_PER_STEP = 8


def _cparams(n_axes):
    return pltpu.CompilerParams(dimension_semantics=("arbitrary",) * n_axes,
                                vmem_limit_bytes=VMEM_LIMIT_BYTES)


def _const_spec(shape):
    zeros = (0,) * len(shape)
    return pl.BlockSpec(shape, lambda *_: zeros, pipeline_mode=pl.Buffered(1))


def _dot(a, b):
    return jnp.dot(a, b, preferred_element_type=F32)


def _dot_nt(a, b):
    return lax.dot_general(a, b, (((1,), (1,)), ((), ())), preferred_element_type=F32)


def _rms(x, g):
    ms = jnp.mean(x * x, axis=-1, keepdims=True)
    return x * lax.rsqrt(ms + EPS) * g


def _dot_f32_by_indicator(x, e):
    hi = x.astype(BF16)
    lo = (x - hi.astype(F32)).astype(BF16)
    return _dot(hi, e) + _dot(lo, e)


def _even_pre_kernel(x_ref, gmix_ref, win_ref, gv_ref, wmix_ref, bmix_ref, gcq_ref, wuq_ref,
                     gckv_ref, wukv_ref, gqn_ref, gknn_ref, gknr_ref, cs_ref, sn_ref, eq_ref,
                     ek_ref, *out_refs, sample, groups, heads, nope, rope_d, q_rank, kv_rank):
    if sample:
        oa_ref, vout_ref, lat_ref, rope_ref, q_ref = out_refs
    else:
        oa_ref, lat_ref, rope_ref, q_ref, k_ref, v_ref = out_refs
    tm = x_ref.shape[0]
    a_w = gv_ref.shape[1]
    gw = a_w // groups
    mb = wmix_ref.shape[1]
    hd = nope + rope_d
    n_nope = heads * nope
    n_rope = heads * rope_d

    x = x_ref[...]
    h = _rms(x, gmix_ref[...]).astype(BF16)
    z = _dot(h, win_ref[...])

    u = jax.nn.gelu(z[:, :a_w])
    v = jax.nn.gelu(z[:, a_w:2 * a_w])
    parts = []
    for g in range(groups):
        vg = v[:, g * gw:(g + 1) * gw]
        parts.append(vg * lax.rsqrt(jnp.mean(vg * vg, axis=-1, keepdims=True) + EPS))
    vn = jnp.concatenate(parts, axis=-1) * gv_ref[...]
    if sample:
        vout_ref[...] = vn
    for c in range(tm // mb):
        rows = slice(c * mb, (c + 1) * mb)
        for g in range(groups):
            cols = slice(g * gw, (g + 1) * gw)
            mix = _dot(wmix_ref[g], vn[rows, cols].astype(BF16)) + bmix_ref[g]
            oa_ref[rows, cols] = (u[rows, cols] * mix).astype(oa_ref.dtype)

    o = 2 * a_w
    r = _rms(z[:, o:o + q_rank], gcq_ref[...]).astype(BF16)
    qall = _dot(r, wuq_ref[...])
    cs = cs_ref[...]
    sn = sn_ref[...]
    q_nope = qall[:, :n_nope]
    q_rope = qall[:, n_nope:n_nope + n_rope] * cs + qall[:, n_nope + n_rope:] * sn
    qq = jnp.concatenate([q_nope, q_rope], axis=-1)
    ss = _dot_f32_by_indicator(qq * qq, eq_ref[...])
    qq = qq * lax.rsqrt(ss * (1.0 / hd) + EPS) * gqn_ref[...]

    o += q_rank
    c = _rms(z[:, o:o + kv_rank], gckv_ref[...])
    lat_ref[...] = c
    o += kv_rank
    kpe2 = z[:, o:o + 2 * rope_d] * cs[:, :2 * rope_d] + z[:, o + 2 * rope_d:o + 4 * rope_d] * sn[:, :2 * rope_d]
    rope_ref[...] = kpe2[:, :rope_d]

    if sample:
        q_ref[...] = qq
        return

    for hh in range(heads):
        q_ref[hh, :, 0:nope] = qq[:, hh * nope:(hh + 1) * nope].astype(q_ref.dtype)
        q_ref[hh, :, nope:hd] = qq[:, n_nope + hh * rope_d:n_nope + (hh + 1) * rope_d].astype(q_ref.dtype)

    kv = _dot(c.astype(BF16), wukv_ref[...])
    kn = kv[:, :n_nope]
    ssr = jnp.sum(kpe2[:, :rope_d] * kpe2[:, :rope_d], axis=-1, keepdims=True)
    ssk = _dot_f32_by_indicator(kn * kn, ek_ref[...]) + ssr
    inv = lax.rsqrt(ssk * (1.0 / hd) + EPS)
    kn = kn * inv[:, :n_nope] * gknn_ref[...]
    kr = jnp.concatenate([kpe2] * (heads // 2), axis=-1) * inv[:, n_nope:] * gknr_ref[...]
    for hh in range(heads):
        k_ref[hh, :, 0:nope] = kn[:, hh * nope:(hh + 1) * nope].astype(k_ref.dtype)
        k_ref[hh, :, nope:hd] = kr[:, hh * rope_d:(hh + 1) * rope_d].astype(k_ref.dtype)
    v_ref[...] = kv[:, n_nope:].astype(v_ref.dtype)


def _even_pre(x, tabs, w, *, sample, tm, pos_rows):
    m, d = x.shape
    dims = w["dims"]
    heads, nope, rope_d = dims["heads"], dims["nope"], dims["rope_d"]
    a_w, vdim = dims["a_w"], dims["vdim"]
    hd = nope + rope_d
    n_cat = heads * hd
    nt = m // tm
    pos_blocks = pos_rows // tm
    row = lambda i: (i, 0)
    pos = lambda i: (i % pos_blocks, 0)
    consts = [w["g_mix"], w["w_in"], w["g_v"], w["wmix_s" if sample else "wmix_p"],
              w["bmix_s" if sample else "bmix_p"], w["g_cq"], w["w_uq"], w["g_ckv"], w["w_ukv"],
              w["g_qn"], w["g_kn_n"], w["g_kn_r"]]
    in_specs = ([pl.BlockSpec((tm, d), row)] + [_const_spec(c.shape) for c in consts]
                + [pl.BlockSpec((tm, heads * rope_d), pos), pl.BlockSpec((tm, heads * rope_d), pos),
                   _const_spec(w["e_q"].shape), _const_spec(w["e_k"].shape)])
    if sample:
        out_shape = [jax.ShapeDtypeStruct((m, a_w), BF16), jax.ShapeDtypeStruct((m, a_w), F32),
                     jax.ShapeDtypeStruct((m, dims["kv_rank"]), F32), jax.ShapeDtypeStruct((m, rope_d), F32),
                     jax.ShapeDtypeStruct((m, n_cat), F32)]
        out_specs = [pl.BlockSpec((tm, a_w), row), pl.BlockSpec((tm, a_w), row),
                     pl.BlockSpec((tm, dims["kv_rank"]), row), pl.BlockSpec((tm, rope_d), row),
                     pl.BlockSpec((tm, n_cat), row)]
    else:
        out_shape = [jax.ShapeDtypeStruct((m, a_w), BF16),
                     jax.ShapeDtypeStruct((m, dims["kv_rank"]), F32), jax.ShapeDtypeStruct((m, rope_d), F32),
                     jax.ShapeDtypeStruct((heads, m, hd), BF16), jax.ShapeDtypeStruct((heads, m, hd), BF16),
                     jax.ShapeDtypeStruct((m, heads * vdim), BF16)]
        out_specs = [pl.BlockSpec((tm, a_w), row),
                     pl.BlockSpec((tm, dims["kv_rank"]), row), pl.BlockSpec((tm, rope_d), row),
                     pl.BlockSpec((heads, tm, hd), lambda i: (0, i, 0)),
                     pl.BlockSpec((heads, tm, hd), lambda i: (0, i, 0)),
                     pl.BlockSpec((tm, heads * vdim), row)]
    kern = functools.partial(_even_pre_kernel, sample=sample, groups=dims["groups"], heads=heads,
                             nope=nope, rope_d=rope_d, q_rank=dims["q_rank"], kv_rank=dims["kv_rank"])
    return pl.pallas_call(
        kern, grid=(nt,), in_specs=in_specs, out_specs=out_specs, out_shape=out_shape,
        compiler_params=_cparams(1), name="even_pre_s" if sample else "even_pre_p",
    )(x, *consts, tabs["mla_cs"], tabs["mla_sn"], w["e_q"], w["e_k"])


def _group_norm_rope(q, g, cs, sn, dh):
    tm, n = q.shape
    lane = lax.broadcasted_iota(jnp.int32, (tm, LANES), 1)
    per_vreg = LANES // dh
    cols = []
    for c in range(n // LANES):
        qc = q[:, c * LANES:(c + 1) * LANES]
        sq = qc * qc
        inv = jnp.zeros_like(qc)
        for s in range(per_vreg):
            sel = (lane >= s * dh) & (lane < (s + 1) * dh)
            ms = jnp.sum(jnp.where(sel, sq, 0.0), axis=-1, keepdims=True) * (1.0 / dh)
            inv = jnp.where(sel, lax.rsqrt(ms + EPS), inv)
        cols.append(qc * inv)
    qn = jnp.concatenate(cols, axis=-1) * g
    half = dh // 8
    lane_d = lax.broadcasted_iota(jnp.int32, (tm, n), 1) % dh
    partner = jnp.where(lane_d < half, pltpu.roll(qn, n - half, 1), pltpu.roll(qn, half, 1))
    return qn * cs + partner * sn


def _odd_pre_kernel(x_ref, gmix_ref, win_ref, gq_ref, gk_ref, cs_ref, sn_ref, *rest,
                    sample, c_w, qk_w, dh, tiles_per_batch):
    if sample:
        xc_ref, q_ref, k_ref, v_ref = rest
    else:
        (cw_ref, cb_ref, lng_ref, lnb_ref,
         oc_ref, q_ref, kf_ref, kb_ref, vf_ref, vb_ref, st_ref, xbuf) = rest
    tm = x_ref.shape[0]
    x = x_ref[...]
    h = _rms(x, gmix_ref[...]).astype(BF16)
    z = _dot(h, win_ref[...])
    xc = z[:, :c_w] * jax.nn.sigmoid(z[:, c_w:2 * c_w])
    o = 2 * c_w
    cs = cs_ref[...]
    sn = sn_ref[...]
    q = _group_norm_rope(z[:, o:o + qk_w], gq_ref[...], cs, sn, dh)
    k = _group_norm_rope(z[:, o + qk_w:o + 2 * qk_w], gk_ref[...], cs, sn, dh)
    v = z[:, o + 2 * qk_w:]
    if sample:
        xc_ref[...] = xc
        q_ref[...] = q
        k_ref[...] = k
        v_ref[...] = v
        return
    q_ref[...] = q.astype(q_ref.dtype)
    kf_ref[...] = k
    kb_ref[...] = k.astype(kb_ref.dtype)
    vf_ref[...] = v
    vb_ref[...] = v.astype(vb_ref.dtype)

    n_taps = cw_ref.shape[0]
    halo = xbuf.shape[0] - tm
    first = (pl.program_id(0) % tiles_per_batch) == 0

    @pl.when(first)
    def _():
        xbuf[0:halo, :] = jnp.zeros((halo, c_w), F32)

    @pl.when(jnp.logical_not(first))
    def _():
        xbuf[0:halo, :] = xbuf[tm:tm + halo, :]

    xbuf[halo:halo + tm, :] = xc
    st_ref[0] = xc[tm - halo:, :]
    y = jnp.zeros((tm, c_w), F32) + cb_ref[...]
    base = halo - (n_taps - 1)
    for t in range(n_taps):
        y = y + cw_ref[t:t + 1, :] * xbuf[base + t:base + t + tm, :]
    mu = jnp.mean(y, axis=-1, keepdims=True)
    dlt = y - mu
    yn = dlt * lax.rsqrt(jnp.mean(dlt * dlt, axis=-1, keepdims=True) + EPS) * lng_ref[...] + lnb_ref[...]
    oc_ref[...] = (yn * jax.nn.sigmoid(yn)).astype(oc_ref.dtype)


def _odd_pre(x, tabs, w, *, sample, tm, pos_rows, tiles_per_batch):
    m, d = x.shape
    dims = w["dims"]
    c_w, qk_w, v_w, dh = dims["c_w"], dims["qk_w"], dims["v_w"], dims["dh"]
    nt = m // tm
    pos_blocks = pos_rows // tm
    row = lambda i: (i, 0)
    pos = lambda i: (i % pos_blocks, 0)
    consts = [w["g_mix"], w["w_in"], w["g_q"], w["g_k"]]
    in_specs = ([pl.BlockSpec((tm, d), row)] + [_const_spec(c.shape) for c in consts]
                + [pl.BlockSpec((tm, qk_w), pos), pl.BlockSpec((tm, qk_w), pos)])
    args = [x] + consts + [tabs["diff_cs"], tabs["diff_sn"]]
    scratch = []
    if sample:
        out_shape = [jax.ShapeDtypeStruct((m, c_w), F32), jax.ShapeDtypeStruct((m, qk_w), F32),
                     jax.ShapeDtypeStruct((m, qk_w), F32), jax.ShapeDtypeStruct((m, v_w), F32)]
        out_specs = [pl.BlockSpec((tm, c_w), row), pl.BlockSpec((tm, qk_w), row),
                     pl.BlockSpec((tm, qk_w), row), pl.BlockSpec((tm, v_w), row)]
    else:
        halo = 32
        conv = [w["conv_w"], w["conv_b"], w["ln_g"], w["ln_b"]]
        in_specs += [_const_spec(c.shape) for c in conv]
        args += conv
        nb = nt // tiles_per_batch
        out_shape = [jax.ShapeDtypeStruct((m, c_w), BF16), jax.ShapeDtypeStruct((m, qk_w), BF16),
                     jax.ShapeDtypeStruct((m, qk_w), F32), jax.ShapeDtypeStruct((m, qk_w), BF16),
                     jax.ShapeDtypeStruct((m, v_w), F32), jax.ShapeDtypeStruct((m, v_w), BF16),
                     jax.ShapeDtypeStruct((nb, halo, c_w), F32)]
        out_specs = [pl.BlockSpec((tm, c_w), row), pl.BlockSpec((tm, qk_w), row),
                     pl.BlockSpec((tm, qk_w), row), pl.BlockSpec((tm, qk_w), row),
                     pl.BlockSpec((tm, v_w), row), pl.BlockSpec((tm, v_w), row),
                     pl.BlockSpec((1, halo, c_w), lambda i: (i // tiles_per_batch, 0, 0))]
        scratch = [pltpu.VMEM((tm + halo, c_w), F32)]
    kern = functools.partial(_odd_pre_kernel, sample=sample, c_w=c_w, qk_w=qk_w, dh=dh,
                             tiles_per_batch=tiles_per_batch)
    return pl.pallas_call(
        kern, grid=(nt,), in_specs=in_specs, out_specs=out_specs, out_shape=out_shape,
        scratch_shapes=scratch, compiler_params=_cparams(1),
        name="odd_pre_s" if sample else "odd_pre_p",
    )(*args)


def _conv_sample_kernel(st_ref, xc_ref, cw_ref, cb_ref, lng_ref, lnb_ref, oc_ref):
    n_state = st_ref.shape[0]
    n_taps = cw_ref.shape[0]
    for t in range(xc_ref.shape[0]):
        y = jnp.zeros(oc_ref.shape[1:], F32) + cb_ref[...]
        for kk in range(n_taps):
            idx = t + kk
            src = st_ref[idx] if idx < n_state else xc_ref[idx - n_state]
            y = y + cw_ref[kk:kk + 1, :] * src
        mu = jnp.mean(y, axis=-1, keepdims=True)
        dlt = y - mu
        yn = dlt * lax.rsqrt(jnp.mean(dlt * dlt, axis=-1, keepdims=True) + EPS) * lng_ref[...] + lnb_ref[...]
        oc_ref[t] = (yn * jax.nn.sigmoid(yn)).astype(oc_ref.dtype)


def _conv_sample(state_t, xc_t, w):
    t, n, c = xc_t.shape
    args = [state_t, xc_t, w["conv_w"], w["conv_b"], w["ln_g"], w["ln_b"]]
    return pl.pallas_call(
        _conv_sample_kernel, grid=(1,), in_specs=[_const_spec(a.shape) for a in args],
        out_specs=pl.BlockSpec((t, n, c), lambda i: (0, 0, 0)), out_shape=jax.ShapeDtypeStruct((t, n, c), BF16),
        compiler_params=_cparams(1), name="conv_sample",
    )(*args)


def _softmax_step(s, m, l):
    m_new = jnp.maximum(m, jnp.max(s, axis=-1, keepdims=True))
    alpha = jnp.exp(m - m_new)
    p = jnp.exp(s - m_new)
    return m_new, alpha, p, alpha * l + jnp.sum(p, axis=-1, keepdims=True)


def _mla_flash_kernel(q_ref, k_ref, v_ref, o_ref, *, heads, vdim, scale):
    tq = q_ref.shape[1]
    i = pl.program_id(1)
    row = lax.broadcasted_iota(jnp.int32, (tq, tq), 0)
    col = lax.broadcasted_iota(jnp.int32, (tq, tq), 1)
    for hh in range(heads):
        q = q_ref[hh]

        def step(j, carry, diag, hh=hh, q=q):
            m, l, acc = carry
            start = pl.multiple_of(j * tq, tq)
            s = _dot_nt(q, k_ref[hh, pl.ds(start, tq), :]) * scale
            if diag:
                s = jnp.where(col <= row, s, NEG_INF)
            m, alpha, p, l = _softmax_step(s, m, l)
            acc = alpha * acc + _dot(p.astype(BF16), v_ref[pl.ds(start, tq), hh * vdim:(hh + 1) * vdim])
            return m, l, acc

        init = (jnp.full((tq, 1), NEG_INF, F32), jnp.zeros((tq, 1), F32), jnp.zeros((tq, vdim), F32))
        carry = lax.fori_loop(0, i, functools.partial(step, diag=False), init)
        m, l, acc = step(i, carry, True)
        o_ref[:, hh * vdim:(hh + 1) * vdim] = (acc / l).astype(o_ref.dtype)


def _mla_flash(q, k, v, *, batch, seq, vdim):
    heads, m, hd = q.shape
    tq = min(ATTN_TILE, seq)
    nq = seq // tq
    kern = functools.partial(_mla_flash_kernel, heads=heads, vdim=vdim, scale=hd ** -0.5)
    return pl.pallas_call(
        kern, grid=(batch, nq),
        in_specs=[pl.BlockSpec((heads, tq, hd), lambda b, i: (0, b * nq + i, 0)),
                  pl.BlockSpec((heads, seq, hd), lambda b, i: (0, b, 0)),
                  pl.BlockSpec((seq, heads * vdim), lambda b, i: (b, 0))],
        out_specs=pl.BlockSpec((tq, heads * vdim), lambda b, i: (b * nq + i, 0)),
        out_shape=jax.ShapeDtypeStruct((m, heads * vdim), BF16),
        compiler_params=_cparams(2), name="mla_flash",
    )(q, k, v)


def _diff_lambda(lq1_ref, lk1_ref, lq2_ref, lk2_ref, lam_init):
    a = jnp.sum(lq1_ref[...] * lk1_ref[...], axis=-1, keepdims=True)
    b = jnp.sum(lq2_ref[...] * lk2_ref[...], axis=-1, keepdims=True)
    return jnp.exp(a) - jnp.exp(b) + lam_init


def _diff_finish(o1, o2, lam, gsub, lam_init):
    o = o1 - lam * o2
    return o * lax.rsqrt(jnp.mean(o * o, axis=-1, keepdims=True) + EPS) * gsub * (1.0 - lam_init)


def _diff_flash_kernel(q_ref, k_ref, v_ref, lq1_ref, lk1_ref, lq2_ref, lk2_ref, gsub_ref, o_ref, *,
                       heads, dh, vdim, scale, lam_init):
    tq = q_ref.shape[0]
    i = pl.program_id(1)
    row = lax.broadcasted_iota(jnp.int32, (tq, tq), 0)
    col = lax.broadcasted_iota(jnp.int32, (tq, tq), 1)
    lane = lax.broadcasted_iota(jnp.int32, (tq, 2 * dh), 1)
    lam = _diff_lambda(lq1_ref, lk1_ref, lq2_ref, lk2_ref, lam_init)
    for hh in range(heads):
        qh = q_ref[:, hh * 2 * dh:(hh + 1) * 2 * dh]
        zero = jnp.zeros_like(qh)
        q1 = jnp.where(lane < dh, qh, zero)
        q2 = jnp.where(lane < dh, zero, qh)

        def step(j, carry, diag, hh=hh, q1=q1, q2=q2):
            m1, l1, a1, m2, l2, a2 = carry
            start = pl.multiple_of(j * tq, tq)
            kh = k_ref[pl.ds(start, tq), hh * 2 * dh:(hh + 1) * 2 * dh]
            vh = v_ref[pl.ds(start, tq), hh * vdim:(hh + 1) * vdim]
            s1 = _dot_nt(q1, kh) * scale
            s2 = _dot_nt(q2, kh) * scale
            if diag:
                s1 = jnp.where(col <= row, s1, NEG_INF)
                s2 = jnp.where(col <= row, s2, NEG_INF)
            m1, al1, p1, l1 = _softmax_step(s1, m1, l1)
            m2, al2, p2, l2 = _softmax_step(s2, m2, l2)
            a1 = al1 * a1 + _dot(p1.astype(BF16), vh)
            a2 = al2 * a2 + _dot(p2.astype(BF16), vh)
            return m1, l1, a1, m2, l2, a2

        one = (jnp.full((tq, 1), NEG_INF, F32), jnp.zeros((tq, 1), F32), jnp.zeros((tq, vdim), F32))
        carry = lax.fori_loop(0, i, functools.partial(step, diag=False), one + one)
        m1, l1, a1, m2, l2, a2 = step(i, carry, True)
        out = _diff_finish(a1 / l1, a2 / l2, lam, gsub_ref[...], lam_init)
        o_ref[:, hh * vdim:(hh + 1) * vdim] = out.astype(o_ref.dtype)


def _diff_flash(q, k, v, w, *, batch, seq):
    m, qk_w = q.shape
    dims = w["dims"]
    heads, dh, vdim = dims["heads"], dims["dh"], dims["vdim"]
    tq = min(ATTN_TILE, seq)
    nq = seq // tq
    small = [w["lq1"], w["lk1"], w["lq2"], w["lk2"], w["g_sub"]]
    kern = functools.partial(_diff_flash_kernel, heads=heads, dh=dh, vdim=vdim, scale=dh ** -0.5,
                             lam_init=w["lam_init"])
    return pl.pallas_call(
        kern, grid=(batch, nq),
        in_specs=[pl.BlockSpec((tq, qk_w), lambda b, i: (b * nq + i, 0)),
                  pl.BlockSpec((seq, qk_w), lambda b, i: (b, 0)),
                  pl.BlockSpec((seq, heads * vdim), lambda b, i: (b, 0))]
                 + [_const_spec(a.shape) for a in small],
        out_specs=pl.BlockSpec((tq, heads * vdim), lambda b, i: (b * nq + i, 0)),
        out_shape=jax.ShapeDtypeStruct((m, heads * vdim), BF16),
        compiler_params=_cparams(2), name="diff_flash",
    )(q, k, v, *small)


def _page_specs(block, layer, n_pages, ppstep):
    def make(i):
        return pl.BlockSpec(block, lambda b, g, pt: (layer, pt[b * n_pages + g * ppstep + i], 0, 0))
    return [make(i) for i in range(ppstep)]


def _mla_paged_kernel(pt_ref, q_ref, cnew_ref, rnew_ref, wukt_ref, wuv_ref, gn_ref, gr_ref, *rest,
                      ppstep, heads, nope, rope_d, scale):
    lat_refs = rest[:ppstep]
    rope_refs = rest[ppstep:2 * ppstep]
    o_ref, m_ref, l_ref, acc_ref = rest[2 * ppstep:]
    g = pl.program_id(1)
    hd = nope + rope_d
    n_nope = heads * nope
    q = q_ref[0]
    qn = [q[:, hh * nope:(hh + 1) * nope].astype(BF16) for hh in range(heads)]
    qr = [q[:, n_nope + hh * rope_d:n_nope + (hh + 1) * rope_d].astype(BF16) for hh in range(heads)]

    @pl.when(g == 0)
    def _():
        m_ref[...] = jnp.full(m_ref.shape, NEG_INF, F32)
        l_ref[...] = jnp.zeros(l_ref.shape, F32)
        acc_ref[...] = jnp.zeros(acc_ref.shape, F32)

    def attend(cs, rts, mask):
        cb = [c.astype(BF16) for c in cs]
        scores = [[] for _ in range(heads)]
        for c16, rt in zip(cb, rts):
            knt = _dot_nt(wukt_ref[...], c16)
            ssr = jnp.sum(rt * rt, axis=0, keepdims=True)
            for hh in range(heads):
                kt = knt[hh * nope:(hh + 1) * nope, :]
                ss = jnp.sum(kt * kt, axis=0, keepdims=True) + ssr
                inv = lax.rsqrt(ss * (1.0 / hd) + EPS)
                ktn = (kt * inv * gn_ref[...]).astype(BF16)
                rtn = (rt * inv * gr_ref[...]).astype(BF16)
                scores[hh].append((_dot(qn[hh], ktn) + _dot(qr[hh], rtn)) * scale)
        call = jnp.concatenate(cb, axis=0) if len(cb) > 1 else cb[0]
        for hh in range(heads):
            s = jnp.concatenate(scores[hh], axis=-1) if len(cb) > 1 else scores[hh][0]
            if mask is not None:
                s = jnp.where(mask, s, NEG_INF)
            m, alpha, p, l = _softmax_step(s, m_ref[hh], l_ref[hh])
            m_ref[hh] = m
            l_ref[hh] = l
            acc_ref[hh] = alpha * acc_ref[hh] + _dot(p.astype(BF16), call)

    attend([r[...] for r in lat_refs], [r[...] for r in rope_refs], None)

    @pl.when(g == pl.num_programs(1) - 1)
    def _():
        rows = q.shape[0]
        slots = cnew_ref.shape[1]
        mask = (lax.broadcasted_iota(jnp.int32, (rows, slots), 1)
                <= lax.broadcasted_iota(jnp.int32, (rows, slots), 0))
        attend([cnew_ref[0]], [rnew_ref[0]], mask)
        vdim = wuv_ref.shape[2]
        for hh in range(heads):
            ol = (acc_ref[hh] / l_ref[hh]).astype(BF16)
            o_ref[0, :, hh * vdim:(hh + 1) * vdim] = _dot(ol, wuv_ref[hh])


def _mla_paged(pt, q8, c_new, rt_new, cache_lat, cache_rope_t, w, *, layer):
    ns, rows, n_cat = q8.shape
    dims = w["dims"]
    heads, nope, rope_d, vdim, rank = dims["heads"], dims["nope"], dims["rope_d"], dims["vdim"], dims["kv_rank"]
    slots = cache_lat.shape[2]
    n_pages = pt.shape[0] // ns
    ppstep = min(PAGES_PER_STEP, n_pages)
    ng = n_pages // ppstep
    per_seq = lambda b, g, pt: (b, 0, 0)
    cm = lambda nd: (lambda b, g, pt: (0,) * nd)
    consts = [w["w_uk_t"], w["w_uv"], w["g_kn_n_col"], w["g_kn_r_col"]]
    in_specs = ([pl.BlockSpec((1, rows, n_cat), per_seq),
                 pl.BlockSpec((1, slots, rank), per_seq), pl.BlockSpec((1, rope_d, slots), per_seq)]
                + [pl.BlockSpec(c.shape, cm(c.ndim), pipeline_mode=pl.Buffered(1)) for c in consts]
                + _page_specs((None, None, slots, rank), layer, n_pages, ppstep)
                + _page_specs((None, None, rope_d, slots), layer, n_pages, ppstep))
    kern = functools.partial(_mla_paged_kernel, ppstep=ppstep, heads=heads, nope=nope, rope_d=rope_d,
                             scale=(nope + rope_d) ** -0.5)
    grid_spec = pltpu.PrefetchScalarGridSpec(
        num_scalar_prefetch=1, grid=(ns, ng), in_specs=in_specs,
        out_specs=pl.BlockSpec((1, rows, heads * vdim), per_seq),
        scratch_shapes=[pltpu.VMEM((heads, rows, 1), F32), pltpu.VMEM((heads, rows, 1), F32),
                        pltpu.VMEM((heads, rows, rank), F32)])
    return pl.pallas_call(
        kern, grid_spec=grid_spec, out_shape=jax.ShapeDtypeStruct((ns, rows, heads * vdim), F32),
        compiler_params=_cparams(2), name="mla_paged",
    )(pt, q8, c_new, rt_new, *consts, *([cache_lat] * ppstep), *([cache_rope_t] * ppstep))


def _diff_paged_kernel(pt_ref, q_ref, knew_ref, vnew_ref, lq1_ref, lk1_ref, lq2_ref, lk2_ref, gsub_ref,
                       *rest, ppstep, heads, dh, vdim, scale, lam_init):
    k_refs = rest[:ppstep]
    v_refs = rest[ppstep:2 * ppstep]
    o_ref, m_ref, l_ref, acc_ref = rest[2 * ppstep:]
    g = pl.program_id(1)
    q = q_ref[0]
    rows = q.shape[0]
    lane = lax.broadcasted_iota(jnp.int32, (rows, 2 * dh), 1)
    q16 = []
    for hh in range(heads):
        qh = q[:, hh * 2 * dh:(hh + 1) * 2 * dh]
        zero = jnp.zeros_like(qh)
        q16.append(jnp.concatenate([jnp.where(lane < dh, qh, zero), jnp.where(lane < dh, zero, qh)],
                                   axis=0).astype(BF16))

    @pl.when(g == 0)
    def _():
        m_ref[...] = jnp.full(m_ref.shape, NEG_INF, F32)
        l_ref[...] = jnp.zeros(l_ref.shape, F32)
        acc_ref[...] = jnp.zeros(acc_ref.shape, F32)

    def attend(kt_refs, vv_refs, sl, mask):
        slots = kt_refs[0].shape[-1]
        for hh in range(heads):
            parts = [_dot(q16[hh], r[sl + (slice(hh * 2 * dh, (hh + 1) * 2 * dh), slice(None))].astype(BF16))
                     for r in kt_refs]
            s = (jnp.concatenate(parts, axis=-1) if len(parts) > 1 else parts[0]) * scale
            if mask is not None:
                s = jnp.where(mask, s, NEG_INF)
            m, alpha, p, l = _softmax_step(s, m_ref[hh], l_ref[hh])
            m_ref[hh] = m
            l_ref[hh] = l
            p = p.astype(BF16)
            acc = alpha * acc_ref[hh]
            for i, r in enumerate(vv_refs):
                vh = r[sl + (pl.ds(hh, slots, stride=heads), slice(None))].astype(BF16)
                acc = acc + _dot(p[:, i * slots:(i + 1) * slots], vh)
            acc_ref[hh] = acc

    attend(k_refs, v_refs, (), None)

    @pl.when(g == pl.num_programs(1) - 1)
    def _():
        slots = knew_ref.shape[-1]
        r_i = lax.broadcasted_iota(jnp.int32, (2 * rows, slots), 0) % rows
        mask = lax.broadcasted_iota(jnp.int32, (2 * rows, slots), 1) <= r_i
        attend([knew_ref], [vnew_ref], (0,), mask)
        lam = _diff_lambda(lq1_ref, lk1_ref, lq2_ref, lk2_ref, lam_init)
        for hh in range(heads):
            o = acc_ref[hh] / l_ref[hh]
            out = _diff_finish(o[:rows], o[rows:], lam, gsub_ref[...], lam_init)
            o_ref[0, :, hh * vdim:(hh + 1) * vdim] = out


def _diff_paged(pt, q8, kt_new, v_new, cache_kt, cache_v, w, *, layer):
    ns, rows, qk_w = q8.shape
    dims = w["dims"]
    heads, dh, vdim = dims["heads"], dims["dh"], dims["vdim"]
    slots = cache_kt.shape[3]
    n_pages = pt.shape[0] // ns
    ppstep = min(PAGES_PER_STEP, n_pages)
    ng = n_pages // ppstep
    per_seq = lambda b, g, pt: (b, 0, 0)
    cm = lambda nd: (lambda b, g, pt: (0,) * nd)
    small = [w["lq1"], w["lk1"], w["lq2"], w["lk2"], w["g_sub"]]
    in_specs = ([pl.BlockSpec((1, rows, qk_w), per_seq),
                 pl.BlockSpec((1, qk_w, slots), per_seq), pl.BlockSpec((1, slots * heads, vdim), per_seq)]
                + [pl.BlockSpec(c.shape, cm(c.ndim), pipeline_mode=pl.Buffered(1)) for c in small]
                + _page_specs((None, None, qk_w, slots), layer, n_pages, ppstep)
                + _page_specs((None, None, slots * heads, vdim), layer, n_pages, ppstep))
    kern = functools.partial(_diff_paged_kernel, ppstep=ppstep, heads=heads, dh=dh, vdim=vdim,
                             scale=dh ** -0.5, lam_init=w["lam_init"])
    grid_spec = pltpu.PrefetchScalarGridSpec(
        num_scalar_prefetch=1, grid=(ns, ng), in_specs=in_specs,
        out_specs=pl.BlockSpec((1, rows, heads * vdim), per_seq),
        scratch_shapes=[pltpu.VMEM((heads, 2 * rows, 1), F32), pltpu.VMEM((heads, 2 * rows, 1), F32),
                        pltpu.VMEM((heads, 2 * rows, vdim), F32)])
    return pl.pallas_call(
        kern, grid_spec=grid_spec, out_shape=jax.ShapeDtypeStruct((ns, rows, heads * vdim), F32),
        compiler_params=_cparams(2), name="diff_paged",
    )(pt, q8, kt_new, v_new, *small, *([cache_kt] * ppstep), *([cache_v] * ppstep))


def _post_kernel(x_ref, oa_ref, ob_ref, wo_ref, gffn_ref, wup_ref, cw_ref, cb_ref, wdn_ref, *rest,
                 sample, d_ff, fc, tiles_per_batch, dec_seq):
    if sample:
        s1_ref, s2_ref, xo_ref, g_ref, gbuf = rest
    else:
        xo_ref, st_ref, gbuf, carry = rest
    tm = x_ref.shape[0]
    half = oa_ref.shape[1]
    pad = SUBLANES
    x1 = x_ref[...] + _dot(oa_ref[...], wo_ref[0:half, :]) + _dot(ob_ref[...], wo_ref[half:, :])
    h = _rms(x1, gffn_ref[...]).astype(BF16)
    acc = jnp.zeros(x1.shape, F32)
    if sample:
        t_idx = lax.broadcasted_iota(jnp.int32, (tm, fc), 0) % dec_seq
        gbuf[0:pad, :] = jnp.zeros((pad, fc), F32)
    else:
        @pl.when((pl.program_id(0) % tiles_per_batch) == 0)
        def _():
            carry[...] = jnp.zeros(carry.shape, F32)
    for c in range(d_ff // fc):
        cols = slice(c * fc, (c + 1) * fc)
        gate = _dot(h, wup_ref[:, c * fc:(c + 1) * fc])
        up = _dot(h, wup_ref[:, d_ff + c * fc:d_ff + (c + 1) * fc])
        if sample:
            g_ref[:, cols] = gate
            gbuf[pad:pad + tm, :] = gate
            p1 = jnp.where(t_idx >= 1, gbuf[pad - 1:pad - 1 + tm, :], s1_ref[:, cols])
            p2 = jnp.where(t_idx >= 2, gbuf[pad - 2:pad - 2 + tm, :], s2_ref[:, cols])
        else:
            gbuf[0:pad, :] = carry[c]
            gbuf[pad:pad + tm, :] = gate
            carry[c] = gate[tm - pad:, :]
            st_ref[0, :, cols] = gate[tm - pad:, :]
            p1 = gbuf[pad - 1:pad - 1 + tm, :]
            p2 = gbuf[pad - 2:pad - 2 + tm, :]
        y = cb_ref[:, cols] + cw_ref[0:1, cols] * p2 + cw_ref[1:2, cols] * p1 + cw_ref[2:3, cols] * gate
        act = (y * jax.nn.sigmoid(y) * up).astype(BF16)
        acc = acc + _dot(act, wdn_ref[cols, :])
    xo_ref[...] = x1 + acc


def _post(x, oa, ob, w_out, fw, *, sample, tm, tiles_per_batch, s1=None, s2=None, dec_seq=1):
    m, d = x.shape
    half = oa.shape[1]
    d_ff = fw["conv_b"].shape[1]
    fc = FFN_CHUNK
    assert d_ff % fc == 0 and fw["conv_w"].shape[0] == 3
    nt = m // tm
    row = lambda i: (i, 0)
    consts = [w_out, fw["g_ffn"], fw["w_up"], fw["conv_w"], fw["conv_b"], fw["w_down"]]
    in_specs = ([pl.BlockSpec((tm, d), row), pl.BlockSpec((tm, half), row), pl.BlockSpec((tm, half), row)]
                + [_const_spec(c.shape) for c in consts])
    args = [x, oa, ob] + consts
    scratch = [pltpu.VMEM((tm + SUBLANES, fc), F32)]
    if sample:
        in_specs += [pl.BlockSpec((tm, d_ff), row), pl.BlockSpec((tm, d_ff), row)]
        args += [s1, s2]
        out_shape = [jax.ShapeDtypeStruct((m, d), F32), jax.ShapeDtypeStruct((m, d_ff), F32)]
        out_specs = [pl.BlockSpec((tm, d), row), pl.BlockSpec((tm, d_ff), row)]
    else:
        nb = nt // tiles_per_batch
        out_shape = [jax.ShapeDtypeStruct((m, d), F32), jax.ShapeDtypeStruct((nb, SUBLANES, d_ff), F32)]
        out_specs = [pl.BlockSpec((tm, d), row),
                     pl.BlockSpec((1, SUBLANES, d_ff), lambda i: (i // tiles_per_batch, 0, 0))]
        scratch.append(pltpu.VMEM((d_ff // fc, SUBLANES, fc), F32))
    kern = functools.partial(_post_kernel, sample=sample, d_ff=d_ff, fc=fc,
                             tiles_per_batch=tiles_per_batch, dec_seq=dec_seq)
    return pl.pallas_call(
        kern, grid=(nt,), in_specs=in_specs, out_specs=out_specs, out_shape=out_shape,
        scratch_shapes=scratch, compiler_params=_cparams(1), name="post_s" if sample else "post_p",
    )(*args)


def _rope_cos_sin(pos, theta, rot):
    half = rot // 2
    inv = jnp.power(theta, -(jnp.arange(half, dtype=F32) * 2.0 / rot))
    ang = pos.astype(F32)[:, None] * inv[None, :]
    return jnp.cos(ang), jnp.sin(ang)


def _tables(pos, heads, rope_d, dh, qk_w):
    cos, sin = _rope_cos_sin(pos, MLA_THETA, rope_d)
    mla_cs = jnp.tile(jnp.concatenate([cos, cos], axis=1), (1, heads))
    mla_sn = jnp.tile(jnp.concatenate([-sin, sin], axis=1), (1, heads))
    rot = dh // 4
    cos, sin = _rope_cos_sin(pos, DIFF_THETA, rot)
    n = pos.shape[0]
    cs = jnp.concatenate([cos, cos, jnp.ones((n, dh - rot), F32)], axis=1)
    sn = jnp.concatenate([-sin, sin, jnp.zeros((n, dh - rot), F32)], axis=1)
    return {"mla_cs": mla_cs, "mla_sn": mla_sn,
            "diff_cs": jnp.tile(cs, (1, qk_w // dh)), "diff_sn": jnp.tile(sn, (1, qk_w // dh))}


def _mix_matrices(w_s, b_s, mb, t):
    mask = jnp.arange(t)[:, None] >= jnp.arange(t)[None, :]
    wt = jnp.where(mask[None], w_s[:, :t, :t], 0.0)
    reps = mb // t
    eye = jnp.eye(reps, dtype=F32)
    wm = jnp.einsum("ab,gts->gatbs", eye, wt).reshape(w_s.shape[0], mb, mb)
    bm = jnp.broadcast_to(jnp.tile(b_s[:, :t], (1, reps))[:, :, None], (w_s.shape[0], mb, LANES))
    return wm.astype(BF16), bm.astype(F32)


def _prep_even(e, p, mb_s, t_s):
    w_in = p["w_in_even"][e]
    g_v = p["gmlp_g_v"][e]
    w_s, b_s = p["gmlp_w_s"][e], p["gmlp_b_s"][e]
    w_uq, w_ukv = p["mla_w_uq"][e], p["mla_w_ukv"][e]
    q_rank, heads, hd = w_uq.shape
    kv_rank = w_ukv.shape[0]
    rope_d = p["cache_mla_rope"].shape[-1]
    nope = hd - rope_d
    vdim = w_ukv.shape[2] - nope
    a_w = g_v.shape[0]
    groups = w_s.shape[0]
    assert a_w // groups == LANES and 2 * rope_d == LANES and heads % 2 == 0
    o = 2 * a_w + q_rank + kv_rank
    kpe = w_in[:, o:o + rope_d]
    half = rope_d // 2
    kpe_sw = jnp.concatenate([kpe[:, half:], kpe[:, :half]], axis=1)
    w_in2 = jnp.concatenate([w_in[:, :o], kpe, kpe, kpe_sw, kpe_sw], axis=1).astype(BF16)
    qr = w_uq[:, :, nope:]
    qr_sw = jnp.concatenate([qr[:, :, half:], qr[:, :, :half]], axis=2)
    w_uq2 = jnp.concatenate([w_uq[:, :, :nope].reshape(q_rank, heads * nope),
                             qr.reshape(q_rank, heads * rope_d),
                             qr_sw.reshape(q_rank, heads * rope_d)], axis=1).astype(BF16)
    w_uk = w_ukv[:, :, :nope]
    w_uv = w_ukv[:, :, nope:]
    w_ukv2 = jnp.concatenate([w_uk.reshape(kv_rank, heads * nope),
                              w_uv.reshape(kv_rank, heads * vdim)], axis=1).astype(BF16)
    hid = jnp.concatenate([jnp.repeat(jnp.arange(heads), nope), jnp.repeat(jnp.arange(heads), rope_d)])
    e_q = (hid[:, None] == hid[None, :]).astype(BF16)
    g_qn, g_kn = p["mla_g_qn"][e], p["mla_g_kn"][e]
    wmix_p, bmix_p = _mix_matrices(w_s, b_s, GMLP_CHUNK, GMLP_CHUNK)
    wmix_s, bmix_s = _mix_matrices(w_s, b_s, mb_s, t_s)
    return {
        "dims": dict(groups=groups, heads=heads, nope=nope, rope_d=rope_d, vdim=vdim, a_w=a_w,
                     q_rank=q_rank, kv_rank=kv_rank),
        "w_in": w_in2, "g_v": g_v[None, :], "wmix_p": wmix_p, "bmix_p": bmix_p,
        "wmix_s": wmix_s, "bmix_s": bmix_s,
        "g_cq": p["mla_g_cq"][e][None, :], "w_uq": w_uq2, "g_ckv": p["mla_g_ckv"][e][None, :],
        "w_ukv": w_ukv2,
        "g_qn": jnp.concatenate([jnp.tile(g_qn[:nope], heads), jnp.tile(g_qn[nope:], heads)])[None, :],
        "g_kn_n": jnp.tile(g_kn[:nope], heads)[None, :], "g_kn_r": jnp.tile(g_kn[nope:], heads)[None, :],
        "e_q": e_q, "e_k": e_q[:heads * nope, :],
        "w_uk_t": jnp.transpose(w_uk, (1, 2, 0)).reshape(heads * nope, kv_rank).astype(BF16),
        "w_uv": jnp.transpose(w_uv, (1, 0, 2)).astype(BF16),
        "g_kn_n_col": jnp.broadcast_to(g_kn[:nope, None], (nope, LANES)),
        "g_kn_r_col": jnp.broadcast_to(g_kn[nope:, None], (rope_d, LANES)),
        "w_out": p["w_out_even"][e].astype(BF16),
    }


def _prep_odd(o, p):
    w_in = p["w_in_odd"][o]
    c_w = p["conv_w"].shape[2]
    _, _, _, heads, _, dh = p["cache_diff_k"].shape
    vdim = p["cache_diff_v"].shape[-1]
    qk_w = heads * 2 * dh
    assert 2 * dh == LANES and vdim == LANES
    layer = 2 * o + 1
    return {
        "dims": dict(heads=heads, dh=dh, vdim=vdim, c_w=c_w, qk_w=qk_w, v_w=heads * vdim),
        "w_in": w_in.astype(BF16),
        "g_q": jnp.tile(p["diff_g_q"][o], qk_w // dh)[None, :],
        "g_k": jnp.tile(p["diff_g_k"][o], qk_w // dh)[None, :],
        "conv_w": p["conv_w"][o], "conv_b": p["conv_b"][o][None, :],
        "ln_g": p["conv_ln_g"][o][None, :], "ln_b": p["conv_ln_b"][o][None, :],
        "lq1": p["diff_lq1"][o][None, :], "lk1": p["diff_lk1"][o][None, :],
        "lq2": p["diff_lq2"][o][None, :], "lk2": p["diff_lk2"][o][None, :],
        "g_sub": p["diff_g_sub"][o][None, :],
        "lam_init": 0.8 - 0.6 * math.exp(-0.3 * layer),
        "w_out": p["w_out_odd"][o].astype(BF16),
    }


def _prep_ffn(l, p):
    return {"g_ffn": p["g_ffn"][l][None, :], "w_up": p["ffn_w_up"][l].astype(BF16),
            "conv_w": p["ffn_conv_w"][l], "conv_b": p["ffn_conv_b"][l][None, :],
            "w_down": p["ffn_w_down"][l].astype(BF16)}


def _pad_rows(a, rows):
    return jnp.pad(a, ((0, 0), (0, rows - a.shape[1]), (0, 0)))


def kernel(x_prompt, x_sample, cache_mla_latent, cache_mla_rope, cache_diff_k, cache_diff_v, state_conv, state_ffn, page_table, g_mix, g_ffn, w_in_even, w_out_even, gmlp_g_v, gmlp_w_s, gmlp_b_s, mla_g_cq, mla_w_uq, mla_g_ckv, mla_w_ukv, mla_g_qn, mla_g_kn, w_in_odd, w_out_odd, conv_w, conv_b, conv_ln_g, conv_ln_b, diff_g_q, diff_g_k, diff_lq1, diff_lk1, diff_lq2, diff_lk2, diff_g_sub, ffn_w_up, ffn_conv_w, ffn_conv_b, ffn_w_down):
    p = dict(cache_mla_rope=cache_mla_rope, cache_diff_k=cache_diff_k, cache_diff_v=cache_diff_v,
             g_ffn=g_ffn, w_in_even=w_in_even, w_out_even=w_out_even, gmlp_g_v=gmlp_g_v,
             gmlp_w_s=gmlp_w_s, gmlp_b_s=gmlp_b_s, mla_g_cq=mla_g_cq, mla_w_uq=mla_w_uq,
             mla_g_ckv=mla_g_ckv, mla_w_ukv=mla_w_ukv, mla_g_qn=mla_g_qn, mla_g_kn=mla_g_kn,
             w_in_odd=w_in_odd, w_out_odd=w_out_odd, conv_w=conv_w, conv_b=conv_b,
             conv_ln_g=conv_ln_g, conv_ln_b=conv_ln_b, diff_g_q=diff_g_q, diff_g_k=diff_g_k,
             diff_lq1=diff_lq1, diff_lk1=diff_lk1, diff_lq2=diff_lq2, diff_lk2=diff_lk2,
             diff_g_sub=diff_g_sub, ffn_w_up=ffn_w_up, ffn_conv_w=ffn_conv_w, ffn_conv_b=ffn_conv_b,
             ffn_w_down=ffn_w_down)
    batch, seq, d = x_prompt.shape
    ns, t_s, _ = x_sample.shape
    depth = g_mix.shape[0]
    n_pages, slots = page_table.shape[1], cache_mla_latent.shape[2]
    past = n_pages * slots
    mp, ms = batch * seq, ns * t_s
    tm_p = min(TOKEN_TILE, seq)
    tm_s = ms
    mb_s = min(GMLP_CHUNK, ms)
    q_rows = SUBLANES
    n_conv_state = state_conv.shape[2]
    assert seq % tm_p == 0 and tm_p % GMLP_CHUNK == 0 and ms % mb_s == 0 and mb_s % t_s == 0
    assert 2 <= t_s <= q_rows and t_s <= slots and t_s <= n_conv_state and seq >= 32 and past % GMLP_CHUNK == 0
    tpb = seq // tm_p

    heads_d, dh = cache_diff_k.shape[3], cache_diff_k.shape[5]
    qk_w = heads_d * 2 * dh
    heads_m, rope_d = mla_w_uq.shape[2], cache_mla_rope.shape[-1]
    tabs_p = _tables(jnp.arange(seq, dtype=jnp.int32), heads_m, rope_d, dh, qk_w)
    tabs_s = _tables(jnp.tile(past + jnp.arange(t_s, dtype=jnp.int32), ns), heads_m, rope_d, dh, qk_w)

    cache_rope_t = jnp.transpose(cache_mla_rope, (0, 1, 3, 2))
    n_pool = cache_diff_k.shape[1]
    cache_kt = jnp.transpose(cache_diff_k, (0, 1, 3, 4, 5, 2)).reshape(-1, n_pool, qk_w, slots)
    cache_v = cache_diff_v.reshape(-1, n_pool, slots * heads_d, cache_diff_v.shape[-1])
    pt = page_table.reshape(-1)

    xp = x_prompt.reshape(mp, d)
    xs = x_sample.reshape(ms, d)
    outs = {k: [] for k in ("gv_s", "lat_p", "rope_p", "lat_s", "rope_s", "conv_p", "conv_s",
                            "dk_p", "dv_p", "dk_s", "dv_s", "ffn_p", "ffn_s")}
    for l in range(depth):
        fw = _prep_ffn(l, p)
        if l % 2 == 0:
            e = l // 2
            w = _prep_even(e, p, mb_s, t_s)
            w["g_mix"] = g_mix[l][None, :]
            dims = w["dims"]
            oa_p, lat_p, rope_p, q_p, k_p, v_p = _even_pre(xp, tabs_p, w, sample=False, tm=tm_p, pos_rows=seq)
            ob_p = _mla_flash(q_p, k_p, v_p, batch=batch, seq=seq, vdim=dims["vdim"])
            oa_s, gv_s, lat_s, rope_s, q_s = _even_pre(xs, tabs_s, w, sample=True, tm=tm_s, pos_rows=ms)
            q8 = _pad_rows(q_s.reshape(ns, t_s, -1), q_rows)
            c_new = _pad_rows(lat_s.reshape(ns, t_s, -1), slots)
            rt_new = jnp.transpose(_pad_rows(rope_s.reshape(ns, t_s, -1), slots), (0, 2, 1))
            o8 = _mla_paged(pt, q8, c_new, rt_new, cache_mla_latent, cache_rope_t, w, layer=e)
            ob_s = o8[:, :t_s].reshape(ms, -1).astype(BF16)
            outs["gv_s"].append(gv_s.reshape(ns, t_s, -1))
            outs["lat_p"].append(lat_p.reshape(batch, seq, -1))
            outs["rope_p"].append(rope_p.reshape(batch, seq, -1))
            outs["lat_s"].append(lat_s.reshape(ns, t_s, -1))
            outs["rope_s"].append(rope_s.reshape(ns, t_s, -1))
        else:
            o = l // 2
            w = _prep_odd(o, p)
            w["g_mix"] = g_mix[l][None, :]
            dims = w["dims"]
            oa_p, q_p, kf_p, kb_p, vf_p, vb_p, st_p = _odd_pre(xp, tabs_p, w, sample=False, tm=tm_p,
                                                               pos_rows=seq, tiles_per_batch=tpb)
            ob_p = _diff_flash(q_p, kb_p, vb_p, w, batch=batch, seq=seq)
            xc_s, q_s, k_s, v_s = _odd_pre(xs, tabs_s, w, sample=True, tm=tm_s, pos_rows=ms, tiles_per_batch=1)
            state_t = jnp.transpose(state_conv[o], (1, 0, 2))
            xc_t = jnp.transpose(xc_s.reshape(ns, t_s, -1), (1, 0, 2))
            oa_s = jnp.transpose(_conv_sample(state_t, xc_t, w), (1, 0, 2)).reshape(ms, -1)
            q8 = _pad_rows(q_s.reshape(ns, t_s, -1), q_rows)
            kt_new = jnp.transpose(_pad_rows(k_s.reshape(ns, t_s, -1), slots), (0, 2, 1))
            v_new = _pad_rows(v_s.reshape(ns, t_s * heads_d, -1), slots * heads_d)
            o8 = _diff_paged(pt, q8, kt_new, v_new, cache_kt, cache_v, w, layer=o)
            ob_s = o8[:, :t_s].reshape(ms, -1).astype(BF16)
            outs["conv_p"].append(st_p[:, st_p.shape[1] - n_conv_state:, :])
            outs["conv_s"].append(jnp.concatenate([state_conv[o][:, t_s:], xc_s.reshape(ns, t_s, -1)], axis=1))
            outs["dk_p"].append(kf_p.reshape(batch, seq, heads_d, 2, dh))
            outs["dv_p"].append(vf_p.reshape(batch, seq, heads_d, -1))
            outs["dk_s"].append(k_s.reshape(ns, t_s, heads_d, 2, dh))
            outs["dv_s"].append(v_s.reshape(ns, t_s, heads_d, -1))
        xp, ffn_st_p = _post(xp, oa_p, ob_p, w["w_out"], fw, sample=False, tm=tm_p, tiles_per_batch=tpb)
        st = state_ffn[l]
        zero = jnp.zeros((ns, t_s - 1, st.shape[-1]), F32)
        s1 = jnp.concatenate([st[:, 1:2], zero], axis=1).reshape(ms, -1)
        s2 = jnp.concatenate([st, zero[:, 1:]], axis=1).reshape(ms, -1)
        xs, g_s = _post(xs, oa_s, ob_s, w["w_out"], fw, sample=True, tm=tm_s, tiles_per_batch=1,
                        s1=s1, s2=s2, dec_seq=t_s)
        outs["ffn_p"].append(ffn_st_p[:, SUBLANES - 2:, :])
        outs["ffn_s"].append(g_s.reshape(ns, t_s, -1)[:, t_s - 2:, :])
    stk = lambda k: jnp.stack(outs[k])
    return (xp.reshape(batch, seq, d), xs.reshape(ns, t_s, d), stk("gv_s"), stk("lat_p"), stk("rope_p"),
            stk("lat_s"), stk("rope_s"), stk("conv_p"), stk("conv_s"), stk("dk_p"), stk("dv_p"),
            stk("dk_s"), stk("dv_s"), stk("ffn_p"), stk("ffn_s"))
```

```python
import functools
import math

import jax
import jax.numpy as jnp
from jax import lax
from jax.experimental import pallas as pl
from jax.experimental.pallas import tpu as pltpu

F32 = jnp.float32
BF16 = jnp.bfloat16

EPS = 1e-6
NEG_INF = -1e30
MLA_THETA = 10000.0
DIFF_THETA = 500000.0
GMLP_CHUNK = 128
LANES = 128
SUBLANES = 8
VMEM_LIMIT_BYTES = 56 * 1024 * 1024
LOG2_E = math.log2(math.e)
TOKEN_TILE = 512
ATTN_TILE = 256
FFN_CHUNK = 256
PAGES_PER_STEP = 16


def _cparams(n_axes):
    return pltpu.CompilerParams(dimension_semantics=("arbitrary",) * n_axes,
                                vmem_limit_bytes=VMEM_LIMIT_BYTES)


def _const_spec(shape):
    zeros = (0,) * len(shape)
    return pl.BlockSpec(shape, lambda *_: zeros, pipeline_mode=pl.Buffered(1))


def _dot(a, b):
    return jnp.dot(a, b, preferred_element_type=F32)


def _dot_nt(a, b):
    return lax.dot_general(a, b, (((1,), (1,)), ((), ())), preferred_element_type=F32)


def _rms(x, g):
    ms = jnp.mean(x * x, axis=-1, keepdims=True)
    return x * lax.rsqrt(ms + EPS) * g


def _dot_f32_by_indicator(x, e):
    hi = x.astype(BF16)
    lo = (x - hi.astype(F32)).astype(BF16)
    return _dot(hi, e) + _dot(lo, e)


def _even_pre_kernel(x_ref, gmix_ref, win_ref, gv_ref, wmix_ref, bmix_ref, gcq_ref, wuq_ref,
                     gckv_ref, wukv_ref, gqn_ref, gknn_ref, gknr_ref, cs_ref, sn_ref, eq_ref,
                     ek_ref, *out_refs, sample, groups, heads, nope, rope_d, q_rank, kv_rank):
    if sample:
        oa_ref, vout_ref, lat_ref, rope_ref, q_ref = out_refs
    else:
        oa_ref, lat_ref, rope_ref, q_ref, k_ref, v_ref = out_refs
    tm = x_ref.shape[0]
    a_w = gv_ref.shape[1]
    gw = a_w // groups
    mb = wmix_ref.shape[1]
    hd = nope + rope_d
    n_nope = heads * nope
    n_rope = heads * rope_d

    x = x_ref[...]
    h = _rms(x, gmix_ref[...]).astype(BF16)
    z = _dot(h, win_ref[...])

    u = jax.nn.gelu(z[:, :a_w])
    v = jax.nn.gelu(z[:, a_w:2 * a_w])
    parts = []
    for g in range(groups):
        vg = v[:, g * gw:(g + 1) * gw]
        parts.append(vg * lax.rsqrt(jnp.mean(vg * vg, axis=-1, keepdims=True) + EPS))
    vn = jnp.concatenate(parts, axis=-1) * gv_ref[...]
    if sample:
        vout_ref[...] = vn
    for c in range(tm // mb):
        rows = slice(c * mb, (c + 1) * mb)
        for g in range(groups):
            cols = slice(g * gw, (g + 1) * gw)
            mix = _dot(wmix_ref[g], vn[rows, cols].astype(BF16)) + bmix_ref[g]
            oa_ref[rows, cols] = (u[rows, cols] * mix).astype(oa_ref.dtype)

    o = 2 * a_w
    r = _rms(z[:, o:o + q_rank], gcq_ref[...]).astype(BF16)
    qall = _dot(r, wuq_ref[...])
    cs = cs_ref[...]
    sn = sn_ref[...]
    q_nope = qall[:, :n_nope]
    q_rope = qall[:, n_nope:n_nope + n_rope] * cs + qall[:, n_nope + n_rope:] * sn
    qq = jnp.concatenate([q_nope, q_rope], axis=-1)
    ss = _dot_f32_by_indicator(qq * qq, eq_ref[...])
    qq = qq * lax.rsqrt(ss * (1.0 / hd) + EPS) * gqn_ref[...]

    o += q_rank
    c = _rms(z[:, o:o + kv_rank], gckv_ref[...])
    lat_ref[...] = c
    o += kv_rank
    kpe2 = z[:, o:o + 2 * rope_d] * cs[:, :2 * rope_d] + z[:, o + 2 * rope_d:o + 4 * rope_d] * sn[:, :2 * rope_d]
    rope_ref[...] = kpe2[:, :rope_d]

    if sample:
        q_ref[...] = qq
        return

    pad = jnp.zeros((tm, q_ref.shape[2] - hd), q_ref.dtype)
    for hh in range(heads):
        q_ref[hh, :, 0:nope] = qq[:, hh * nope:(hh + 1) * nope].astype(q_ref.dtype)
        q_ref[hh, :, nope:hd] = qq[:, n_nope + hh * rope_d:n_nope + (hh + 1) * rope_d].astype(q_ref.dtype)
        if pad.shape[1]:
            q_ref[hh, :, hd:] = pad

    kv = _dot(c.astype(BF16), wukv_ref[...])
    kn = kv[:, :n_nope]
    ssr = jnp.sum(kpe2[:, :rope_d] * kpe2[:, :rope_d], axis=-1, keepdims=True)
    ssk = _dot_f32_by_indicator(kn * kn, ek_ref[...]) + ssr
    inv = lax.rsqrt(ssk * (1.0 / hd) + EPS)
    kn = kn * inv[:, :n_nope] * gknn_ref[...]
    kr = jnp.concatenate([kpe2] * (heads // 2), axis=-1) * inv[:, n_nope:] * gknr_ref[...]
    for hh in range(heads):
        k_ref[hh, :, 0:nope] = kn[:, hh * nope:(hh + 1) * nope].astype(k_ref.dtype)
        k_ref[hh, :, nope:hd] = kr[:, hh * rope_d:(hh + 1) * rope_d].astype(k_ref.dtype)
        if pad.shape[1]:
            k_ref[hh, :, hd:] = pad
    v_ref[...] = kv[:, n_nope:].astype(v_ref.dtype)


def _even_pre(x, tabs, w, *, sample, tm, pos_rows):
    m, d = x.shape
    dims = w["dims"]
    heads, nope, rope_d = dims["heads"], dims["nope"], dims["rope_d"]
    a_w, vdim = dims["a_w"], dims["vdim"]
    hd = nope + rope_d
    n_cat = heads * hd
    hdp = -(-hd // LANES) * LANES
    nt = m // tm
    pos_blocks = pos_rows // tm
    row = lambda i: (i, 0)
    pos = lambda i: (i % pos_blocks, 0)
    consts = [w["g_mix"], w["w_in"], w["g_v"], w["wmix_s" if sample else "wmix_p"],
              w["bmix_s" if sample else "bmix_p"], w["g_cq"], w["w_uq"], w["g_ckv"], w["w_ukv"],
              w["g_qn"], w["g_kn_n"], w["g_kn_r"]]
    in_specs = ([pl.BlockSpec((tm, d), row)] + [_const_spec(c.shape) for c in consts]
                + [pl.BlockSpec((tm, heads * rope_d), pos), pl.BlockSpec((tm, heads * rope_d), pos),
                   _const_spec(w["e_q"].shape), _const_spec(w["e_k"].shape)])
    if sample:
        out_shape = [jax.ShapeDtypeStruct((m, a_w), BF16), jax.ShapeDtypeStruct((m, a_w), F32),
                     jax.ShapeDtypeStruct((m, dims["kv_rank"]), F32), jax.ShapeDtypeStruct((m, rope_d), F32),
                     jax.ShapeDtypeStruct((m, n_cat), F32)]
        out_specs = [pl.BlockSpec((tm, a_w), row), pl.BlockSpec((tm, a_w), row),
                     pl.BlockSpec((tm, dims["kv_rank"]), row), pl.BlockSpec((tm, rope_d), row),
                     pl.BlockSpec((tm, n_cat), row)]
    else:
        out_shape = [jax.ShapeDtypeStruct((m, a_w), BF16),
                     jax.ShapeDtypeStruct((m, dims["kv_rank"]), F32), jax.ShapeDtypeStruct((m, rope_d), F32),
                     jax.ShapeDtypeStruct((heads, m, hdp), BF16), jax.ShapeDtypeStruct((heads, m, hdp), BF16),
                     jax.ShapeDtypeStruct((m, heads * vdim), BF16)]
        out_specs = [pl.BlockSpec((tm, a_w), row),
                     pl.BlockSpec((tm, dims["kv_rank"]), row), pl.BlockSpec((tm, rope_d), row),
                     pl.BlockSpec((heads, tm, hdp), lambda i: (0, i, 0)),
                     pl.BlockSpec((heads, tm, hdp), lambda i: (0, i, 0)),
                     pl.BlockSpec((tm, heads * vdim), row)]
    kern = functools.partial(_even_pre_kernel, sample=sample, groups=dims["groups"], heads=heads,
                             nope=nope, rope_d=rope_d, q_rank=dims["q_rank"], kv_rank=dims["kv_rank"])
    return pl.pallas_call(
        kern, grid=(nt,), in_specs=in_specs, out_specs=out_specs, out_shape=out_shape,
        compiler_params=_cparams(1), name="even_pre_s" if sample else "even_pre_p",
    )(x, *consts, tabs["mla_cs"], tabs["mla_sn"], w["e_q"], w["e_k"])


def _group_norm_rope(q, g, cs, sn, dh):
    tm, n = q.shape
    lane = lax.broadcasted_iota(jnp.int32, (tm, LANES), 1)
    per_vreg = LANES // dh
    cols = []
    for c in range(n // LANES):
        qc = q[:, c * LANES:(c + 1) * LANES]
        sq = qc * qc
        inv = jnp.zeros_like(qc)
        for s in range(per_vreg):
            sel = (lane >= s * dh) & (lane < (s + 1) * dh)
            ms = jnp.sum(jnp.where(sel, sq, 0.0), axis=-1, keepdims=True) * (1.0 / dh)
            inv = jnp.where(sel, lax.rsqrt(ms + EPS), inv)
        cols.append(qc * inv)
    qn = jnp.concatenate(cols, axis=-1) * g
    half = dh // 8
    lane_d = lax.broadcasted_iota(jnp.int32, (tm, n), 1) % dh
    partner = jnp.where(lane_d < half, pltpu.roll(qn, n - half, 1), pltpu.roll(qn, half, 1))
    return qn * cs + partner * sn


def _odd_pre_kernel(x_ref, gmix_ref, win_ref, gq_ref, gk_ref, cs_ref, sn_ref, *rest,
                    sample, c_w, qk_w, dh, tiles_per_batch):
    if sample:
        xc_ref, q_ref, k_ref, v_ref = rest
    else:
        (cw_ref, cb_ref, lng_ref, lnb_ref,
         oc_ref, q_ref, kf_ref, kb_ref, vf_ref, vb_ref, st_ref, xbuf) = rest
    tm = x_ref.shape[0]
    x = x_ref[...]
    h = _rms(x, gmix_ref[...]).astype(BF16)
    z = _dot(h, win_ref[...])
    xc = z[:, :c_w] * jax.nn.sigmoid(z[:, c_w:2 * c_w])
    o = 2 * c_w
    cs = cs_ref[...]
    sn = sn_ref[...]
    q = _group_norm_rope(z[:, o:o + qk_w], gq_ref[...], cs, sn, dh)
    k = _group_norm_rope(z[:, o + qk_w:o + 2 * qk_w], gk_ref[...], cs, sn, dh)
    v = z[:, o + 2 * qk_w:]
    if sample:
        xc_ref[...] = xc
        q_ref[...] = q
        k_ref[...] = k
        v_ref[...] = v
        return
    q_ref[...] = q.astype(q_ref.dtype)
    kf_ref[...] = k
    kb_ref[...] = k.astype(kb_ref.dtype)
    vf_ref[...] = v
    vb_ref[...] = v.astype(vb_ref.dtype)

    n_taps = cw_ref.shape[0]
    halo = xbuf.shape[0] - tm
    first = (pl.program_id(0) % tiles_per_batch) == 0

    @pl.when(first)
    def _():
        xbuf[0:halo, :] = jnp.zeros((halo, c_w), F32)

    @pl.when(jnp.logical_not(first))
    def _():
        xbuf[0:halo, :] = xbuf[tm:tm + halo, :]

    xbuf[halo:halo + tm, :] = xc
    st_ref[0] = xc[tm - halo:, :]
    y = jnp.zeros((tm, c_w), F32) + cb_ref[...]
    base = halo - (n_taps - 1)
    for t in range(n_taps):
        y = y + cw_ref[t:t + 1, :] * xbuf[base + t:base + t + tm, :]
    mu = jnp.mean(y, axis=-1, keepdims=True)
    dlt = y - mu
    yn = dlt * lax.rsqrt(jnp.mean(dlt * dlt, axis=-1, keepdims=True) + EPS) * lng_ref[...] + lnb_ref[...]
    oc_ref[...] = (yn * jax.nn.sigmoid(yn)).astype(oc_ref.dtype)


def _odd_pre(x, tabs, w, *, sample, tm, pos_rows, tiles_per_batch):
    m, d = x.shape
    dims = w["dims"]
    c_w, qk_w, v_w, dh = dims["c_w"], dims["qk_w"], dims["v_w"], dims["dh"]
    nt = m // tm
    pos_blocks = pos_rows // tm
    row = lambda i: (i, 0)
    pos = lambda i: (i % pos_blocks, 0)
    consts = [w["g_mix"], w["w_in"], w["g_q"], w["g_k"]]
    in_specs = ([pl.BlockSpec((tm, d), row)] + [_const_spec(c.shape) for c in consts]
                + [pl.BlockSpec((tm, qk_w), pos), pl.BlockSpec((tm, qk_w), pos)])
    args = [x] + consts + [tabs["diff_cs"], tabs["diff_sn"]]
    scratch = []
    if sample:
        out_shape = [jax.ShapeDtypeStruct((m, c_w), F32), jax.ShapeDtypeStruct((m, qk_w), F32),
                     jax.ShapeDtypeStruct((m, qk_w), F32), jax.ShapeDtypeStruct((m, v_w), F32)]
        out_specs = [pl.BlockSpec((tm, c_w), row), pl.BlockSpec((tm, qk_w), row),
                     pl.BlockSpec((tm, qk_w), row), pl.BlockSpec((tm, v_w), row)]
    else:
        halo = 32
        conv = [w["conv_w"], w["conv_b"], w["ln_g"], w["ln_b"]]
        in_specs += [_const_spec(c.shape) for c in conv]
        args += conv
        nb = nt // tiles_per_batch
        out_shape = [jax.ShapeDtypeStruct((m, c_w), BF16), jax.ShapeDtypeStruct((m, qk_w), BF16),
                     jax.ShapeDtypeStruct((m, qk_w), F32), jax.ShapeDtypeStruct((m, qk_w), BF16),
                     jax.ShapeDtypeStruct((m, v_w), F32), jax.ShapeDtypeStruct((m, v_w), BF16),
                     jax.ShapeDtypeStruct((nb, halo, c_w), F32)]
        out_specs = [pl.BlockSpec((tm, c_w), row), pl.BlockSpec((tm, qk_w), row),
                     pl.BlockSpec((tm, qk_w), row), pl.BlockSpec((tm, qk_w), row),
                     pl.BlockSpec((tm, v_w), row), pl.BlockSpec((tm, v_w), row),
                     pl.BlockSpec((1, halo, c_w), lambda i: (i // tiles_per_batch, 0, 0))]
        scratch = [pltpu.VMEM((tm + halo, c_w), F32)]
    kern = functools.partial(_odd_pre_kernel, sample=sample, c_w=c_w, qk_w=qk_w, dh=dh,
                             tiles_per_batch=tiles_per_batch)
    return pl.pallas_call(
        kern, grid=(nt,), in_specs=in_specs, out_specs=out_specs, out_shape=out_shape,
        scratch_shapes=scratch, compiler_params=_cparams(1),
        name="odd_pre_s" if sample else "odd_pre_p",
    )(*args)


def _conv_sample_kernel(st_ref, xc_ref, cw_ref, cb_ref, lng_ref, lnb_ref, oc_ref):
    n_state = st_ref.shape[0]
    n_taps = cw_ref.shape[0]
    for t in range(xc_ref.shape[0]):
        y = jnp.zeros(oc_ref.shape[1:], F32) + cb_ref[...]
        for kk in range(n_taps):
            idx = t + kk
            src = st_ref[idx] if idx < n_state else xc_ref[idx - n_state]
            y = y + cw_ref[kk:kk + 1, :] * src
        mu = jnp.mean(y, axis=-1, keepdims=True)
        dlt = y - mu
        yn = dlt * lax.rsqrt(jnp.mean(dlt * dlt, axis=-1, keepdims=True) + EPS) * lng_ref[...] + lnb_ref[...]
        oc_ref[t] = (yn * jax.nn.sigmoid(yn)).astype(oc_ref.dtype)


def _conv_sample(state_t, xc_t, w):
    t, n, c = xc_t.shape
    args = [state_t, xc_t, w["conv_w"], w["conv_b"], w["ln_g"], w["ln_b"]]
    return pl.pallas_call(
        _conv_sample_kernel, grid=(1,), in_specs=[_const_spec(a.shape) for a in args],
        out_specs=pl.BlockSpec((t, n, c), lambda i: (0, 0, 0)), out_shape=jax.ShapeDtypeStruct((t, n, c), BF16),
        compiler_params=_cparams(1), name="conv_sample",
    )(*args)


def _softmax_step(s, m, l):
    m_new = jnp.maximum(m, jnp.max(s, axis=-1, keepdims=True))
    alpha = jnp.exp(m - m_new)
    p = jnp.exp(s - m_new)
    return m_new, alpha, p, alpha * l + jnp.sum(p, axis=-1, keepdims=True)


def _flash_t_update(st, idx, vt, m_ref, l_ref, acc_ref, c):
    m_prev = m_ref[idx]
    m_new = jnp.maximum(m_prev, jnp.max(st, axis=0, keepdims=True))
    alpha = jnp.exp2((m_prev - m_new) * c)
    p = jnp.exp2((st - m_new) * c)
    m_ref[idx] = m_new
    l_ref[idx] = alpha * l_ref[idx] + jnp.sum(p, axis=0, keepdims=True)
    acc_ref[idx] = alpha * acc_ref[idx] + _dot(vt, p.astype(BF16))


def _flash_init(m_ref, l_ref, acc_ref):
    m_ref[...] = jnp.full(m_ref.shape, NEG_INF, F32)
    l_ref[...] = jnp.zeros(l_ref.shape, F32)
    acc_ref[...] = jnp.zeros(acc_ref.shape, F32)


def _build_vt(v_ref, vt_ref, heads, vdim):
    tk = vt_ref.shape[-1]
    for hh in range(heads):
        for jb in range(vt_ref.shape[1]):
            blk = v_ref[jb * tk:(jb + 1) * tk, hh * vdim:(hh + 1) * vdim].astype(F32)
            vt_ref[hh, jb] = blk.T.astype(vt_ref.dtype)


def _diag_mask(tk, strip, st):
    key = lax.broadcasted_iota(jnp.int32, (tk, strip), 0)
    qry = lax.broadcasted_iota(jnp.int32, (tk, strip), 1) + st * strip
    return key <= qry


def _mla_flash_kernel(q_ref, k_ref, v_ref, o_ref, vt_ref, m_ref, l_ref, acc_ref, *, heads, vdim, c, strip):
    tq = q_ref.shape[1]
    n_strip = tq // strip
    i = pl.program_id(1)

    @pl.when(i == 0)
    def _():
        _build_vt(v_ref, vt_ref, heads, vdim)

    _flash_init(m_ref, l_ref, acc_ref)

    def block(j, diag):
        start = pl.multiple_of(j * tq, tq)
        for hh in range(heads):
            kb = k_ref[hh, pl.ds(start, tq), :]
            vt = vt_ref[hh, j]
            for st in range(n_strip):
                s_t = _dot_nt(kb, q_ref[hh, st * strip:(st + 1) * strip, :])
                if diag:
                    s_t = jnp.where(_diag_mask(tq, strip, st), s_t, NEG_INF)
                _flash_t_update(s_t, hh * n_strip + st, vt, m_ref, l_ref, acc_ref, c)

    def body(j, carry):
        block(j, False)
        return carry

    lax.fori_loop(0, i, body, 0)
    block(i, True)
    for hh in range(heads):
        for st in range(n_strip):
            idx = hh * n_strip + st
            o_t = acc_ref[idx] / l_ref[idx]
            o_ref[st * strip:(st + 1) * strip, hh * vdim:(hh + 1) * vdim] = o_t.T.astype(o_ref.dtype)


def _flash_scratch(n_chain, heads, seq, tq, vdim, strip):
    return [pltpu.VMEM((heads, seq // tq, vdim, tq), BF16),
            pltpu.VMEM((n_chain, 1, strip), F32), pltpu.VMEM((n_chain, 1, strip), F32),
            pltpu.VMEM((n_chain, vdim, strip), F32)]


def _mla_flash(q, k, v, *, batch, seq, vdim, scale):
    heads, m, hd = q.shape
    tq = min(ATTN_TILE, seq)
    nq = seq // tq
    strip = LANES
    kern = functools.partial(_mla_flash_kernel, heads=heads, vdim=vdim, c=scale * LOG2_E, strip=strip)
    return pl.pallas_call(
        kern, grid=(batch, nq),
        in_specs=[pl.BlockSpec((heads, tq, hd), lambda b, i: (0, b * nq + i, 0)),
                  pl.BlockSpec((heads, seq, hd), lambda b, i: (0, b, 0)),
                  pl.BlockSpec((seq, heads * vdim), lambda b, i: (b, 0))],
        out_specs=pl.BlockSpec((tq, heads * vdim), lambda b, i: (b * nq + i, 0)),
        out_shape=jax.ShapeDtypeStruct((m, heads * vdim), BF16),
        scratch_shapes=_flash_scratch(heads * (tq // strip), heads, seq, tq, vdim, strip),
        compiler_params=_cparams(2), name="mla_flash",
    )(q, k, v)


def _diff_lambda(lq1_ref, lk1_ref, lq2_ref, lk2_ref, lam_init):
    a = jnp.sum(lq1_ref[...] * lk1_ref[...], axis=-1, keepdims=True)
    b = jnp.sum(lq2_ref[...] * lk2_ref[...], axis=-1, keepdims=True)
    return jnp.exp(a) - jnp.exp(b) + lam_init


def _diff_finish(o1, o2, lam, gsub, lam_init):
    o = o1 - lam * o2
    return o * lax.rsqrt(jnp.mean(o * o, axis=-1, keepdims=True) + EPS) * gsub * (1.0 - lam_init)


def _diff_flash_kernel(q_ref, k_ref, v_ref, lq1_ref, lk1_ref, lq2_ref, lk2_ref, gsub_ref, o_ref,
                       qm_ref, vt_ref, m_ref, l_ref, acc_ref, *, heads, dh, vdim, c, lam_init, strip):
    tq = q_ref.shape[0]
    n_strip = tq // strip
    i = pl.program_id(1)

    @pl.when(i == 0)
    def _():
        _build_vt(v_ref, vt_ref, heads, vdim)

    lane = lax.broadcasted_iota(jnp.int32, (tq, 2 * dh), 1)
    _flash_init(m_ref, l_ref, acc_ref)
    for hh in range(heads):
        qh = q_ref[:, hh * 2 * dh:(hh + 1) * 2 * dh]
        zero = jnp.zeros_like(qh)
        qm_ref[2 * hh] = jnp.where(lane < dh, qh, zero)
        qm_ref[2 * hh + 1] = jnp.where(lane < dh, zero, qh)

    def block(j, diag):
        start = pl.multiple_of(j * tq, tq)
        for hh in range(heads):
            kb = k_ref[pl.ds(start, tq), hh * 2 * dh:(hh + 1) * 2 * dh]
            vt = vt_ref[hh, j]
            for comp in range(2):
                for st in range(n_strip):
                    s_t = _dot_nt(kb, qm_ref[2 * hh + comp, st * strip:(st + 1) * strip, :])
                    if diag:
                        s_t = jnp.where(_diag_mask(tq, strip, st), s_t, NEG_INF)
                    _flash_t_update(s_t, (2 * hh + comp) * n_strip + st, vt, m_ref, l_ref, acc_ref, c)

    def body(j, carry):
        block(j, False)
        return carry

    lax.fori_loop(0, i, body, 0)
    block(i, True)
    lam = _diff_lambda(lq1_ref, lk1_ref, lq2_ref, lk2_ref, lam_init)
    for hh in range(heads):
        for st in range(n_strip):
            i1 = (2 * hh) * n_strip + st
            i2 = (2 * hh + 1) * n_strip + st
            o1 = (acc_ref[i1] / l_ref[i1]).T
            o2 = (acc_ref[i2] / l_ref[i2]).T
            out = _diff_finish(o1, o2, lam, gsub_ref[...], lam_init)
            o_ref[st * strip:(st + 1) * strip, hh * vdim:(hh + 1) * vdim] = out.astype(o_ref.dtype)


def _diff_flash(q, k, v, w, *, batch, seq):
    m, qk_w = q.shape
    dims = w["dims"]
    heads, dh, vdim = dims["heads"], dims["dh"], dims["vdim"]
    tq = min(ATTN_TILE, seq)
    nq = seq // tq
    small = [w["lq1"], w["lk1"], w["lq2"], w["lk2"], w["g_sub"]]
    strip = LANES
    kern = functools.partial(_diff_flash_kernel, heads=heads, dh=dh, vdim=vdim, c=dh ** -0.5 * LOG2_E,
                             lam_init=w["lam_init"], strip=strip)
    return pl.pallas_call(
        kern, grid=(batch, nq),
        in_specs=[pl.BlockSpec((tq, qk_w), lambda b, i: (b * nq + i, 0)),
                  pl.BlockSpec((seq, qk_w), lambda b, i: (b, 0)),
                  pl.BlockSpec((seq, heads * vdim), lambda b, i: (b, 0))]
                 + [_const_spec(a.shape) for a in small],
        out_specs=pl.BlockSpec((tq, heads * vdim), lambda b, i: (b * nq + i, 0)),
        out_shape=jax.ShapeDtypeStruct((m, heads * vdim), BF16),
        scratch_shapes=[pltpu.VMEM((2 * heads, tq, 2 * dh), BF16)]
                       + _flash_scratch(2 * heads * (tq // strip), heads, seq, tq, vdim, strip),
        compiler_params=_cparams(2), name="diff_flash",
    )(q, k, v, *small)


def _page_specs(block, layer, n_pages, ppstep):
    def make(i):
        return pl.BlockSpec(block, lambda b, g, pt: (layer, pt[b * n_pages + g * ppstep + i], 0, 0))
    return [make(i) for i in range(ppstep)]


def _mla_paged_kernel(pt_ref, q_ref, cnew_ref, rnew_ref, wukt_ref, wuv_ref, gn_ref, gr_ref, *rest,
                      ppstep, heads, nope, rope_d, scale):
    lat_refs = rest[:ppstep]
    rope_refs = rest[ppstep:2 * ppstep]
    o_ref, lhs_ref, qr_ref, m_ref, l_ref, acc_ref = rest[2 * ppstep:]
    g = pl.program_id(1)
    hd = nope + rope_d
    n_nope = heads * nope
    rows = q_ref.shape[1]

    @pl.when(g == 0)
    def _():
        q = q_ref[0]
        lhs_ref[0:n_nope, :] = wukt_ref[...]
        qa, qr = [], []
        for hh in range(heads):
            qg = (q[:, hh * nope:(hh + 1) * nope] * gn_ref[...]).astype(BF16)
            qa.append(_dot(qg, wukt_ref[hh * nope:(hh + 1) * nope, :]))
            qr.append(q[:, n_nope + hh * rope_d:n_nope + (hh + 1) * rope_d] * gr_ref[...])
        lhs_ref[n_nope:, :] = jnp.concatenate(qa, axis=0).astype(BF16)
        qr_ref[...] = jnp.concatenate(qr, axis=0).astype(BF16)
        _flash_init(m_ref, l_ref, acc_ref)

    def attend(cs, rts, mask):
        cb = [c.astype(BF16) for c in cs]
        scores = []
        for c16, rt in zip(cb, rts):
            slots = c16.shape[0]
            big = _dot_nt(lhs_ref[...], c16)
            ssr = jnp.sum(rt * rt, axis=0, keepdims=True)
            inv = []
            for hh in range(heads):
                kt = big[hh * nope:(hh + 1) * nope, :]
                ss = jnp.sum(kt * kt, axis=0, keepdims=True) + ssr
                inv.append(jnp.broadcast_to(lax.rsqrt(ss * (1.0 / hd) + EPS) * scale, (rows, slots)))
            s = big[n_nope:, :] + _dot(qr_ref[...], rt.astype(BF16))
            scores.append(s * jnp.concatenate(inv, axis=0))
        scores[-1] = jnp.where(mask, scores[-1], NEG_INF)
        s = jnp.concatenate(scores, axis=-1)
        m, alpha, p, l = _softmax_step(s, m_ref[...], l_ref[...])
        m_ref[...] = m
        l_ref[...] = l
        acc_ref[...] = alpha * acc_ref[...] + _dot(p.astype(BF16), jnp.concatenate(cb, axis=0))

    is_last = g == pl.num_programs(1) - 1
    slots = cnew_ref.shape[1]
    r_i = lax.broadcasted_iota(jnp.int32, (heads * rows, slots), 0) % rows
    new_mask = (lax.broadcasted_iota(jnp.int32, (heads * rows, slots), 1) <= r_i) & is_last
    attend([r[...] for r in lat_refs] + [cnew_ref[0]], [r[...] for r in rope_refs] + [rnew_ref[0]], new_mask)

    @pl.when(is_last)
    def _():
        vdim = wuv_ref.shape[2]
        ol = (acc_ref[...] / l_ref[...]).astype(BF16)
        for hh in range(heads):
            o_ref[0, :, hh * vdim:(hh + 1) * vdim] = _dot(ol[hh * rows:(hh + 1) * rows, :], wuv_ref[hh])


def _mla_paged(pt, q8, c_new, rt_new, cache_lat, cache_rope_t, w, *, layer):
    ns, rows, n_cat = q8.shape
    dims = w["dims"]
    heads, nope, rope_d, vdim, rank = dims["heads"], dims["nope"], dims["rope_d"], dims["vdim"], dims["kv_rank"]
    slots = cache_lat.shape[2]
    n_pages = pt.shape[0] // ns
    ppstep = min(PAGES_PER_STEP, n_pages)
    ng = n_pages // ppstep
    per_seq = lambda b, g, pt: (b, 0, 0)
    cm = lambda nd: (lambda b, g, pt: (0,) * nd)
    consts = [w["w_uk_t"], w["w_uv"], w["g_kn_n"][:, :nope], w["g_kn_r"][:, :rope_d]]
    in_specs = ([pl.BlockSpec((1, rows, n_cat), per_seq),
                 pl.BlockSpec((1, slots, rank), per_seq), pl.BlockSpec((1, rope_d, slots), per_seq)]
                + [pl.BlockSpec(c.shape, cm(c.ndim), pipeline_mode=pl.Buffered(1)) for c in consts]
                + _page_specs((None, None, slots, rank), layer, n_pages, ppstep)
                + _page_specs((None, None, rope_d, slots), layer, n_pages, ppstep))
    kern = functools.partial(_mla_paged_kernel, ppstep=ppstep, heads=heads, nope=nope, rope_d=rope_d,
                             scale=(nope + rope_d) ** -0.5)
    grid_spec = pltpu.PrefetchScalarGridSpec(
        num_scalar_prefetch=1, grid=(ns, ng), in_specs=in_specs,
        out_specs=pl.BlockSpec((1, rows, heads * vdim), per_seq),
        scratch_shapes=[pltpu.VMEM((heads * (nope + rows), rank), BF16),
                        pltpu.VMEM((heads * rows, rope_d), BF16),
                        pltpu.VMEM((heads * rows, 1), F32), pltpu.VMEM((heads * rows, 1), F32),
                        pltpu.VMEM((heads * rows, rank), F32)])
    return pl.pallas_call(
        kern, grid_spec=grid_spec, out_shape=jax.ShapeDtypeStruct((ns, rows, heads * vdim), F32),
        compiler_params=_cparams(2), name="mla_paged",
    )(pt, q8, c_new, rt_new, *consts, *([cache_lat] * ppstep), *([cache_rope_t] * ppstep))


def _diff_paged_kernel(pt_ref, q_ref, knew_ref, vnew_ref, lq1_ref, lk1_ref, lq2_ref, lk2_ref, gsub_ref,
                       *rest, ppstep, heads, dh, vdim, scale, lam_init):
    k_refs = rest[:ppstep]
    v_refs = rest[ppstep:2 * ppstep]
    o_ref, qbd_ref, m_ref, l_ref, acc_ref = rest[2 * ppstep:]
    g = pl.program_id(1)
    rows = q_ref.shape[1]
    n_maps = 2 * heads
    qk_w = n_maps * dh

    @pl.when(g == 0)
    def _():
        q = q_ref[0]
        lane = lax.broadcasted_iota(jnp.int32, (rows, qk_w), 1)
        zero = jnp.zeros_like(q)
        blocks = [jnp.where((lane >= i * dh) & (lane < (i + 1) * dh), q, zero) for i in range(n_maps)]
        qbd_ref[...] = jnp.concatenate(blocks, axis=0).astype(BF16)
        _flash_init(m_ref, l_ref, acc_ref)

    def attend(kt_refs, vv_refs, mask):
        slots = kt_refs[0][0].shape[-1]
        full = (slice(None), slice(None))
        parts = [_dot(qbd_ref[...], r[sl + full].astype(BF16)) * scale for r, sl in kt_refs]
        parts[-1] = jnp.where(mask, parts[-1], NEG_INF)
        s = jnp.concatenate(parts, axis=-1)
        m, alpha, p, l = _softmax_step(s, m_ref[...], l_ref[...])
        m_ref[...] = m
        l_ref[...] = l
        p = p.astype(BF16)
        for hh in range(heads):
            hr = slice(hh * 2 * rows, (hh + 1) * 2 * rows)
            vs = [r[sl + (pl.ds(hh, slots, stride=heads), slice(None))].astype(BF16) for r, sl in vv_refs]
            acc_ref[hr, :] = alpha[hr, :] * acc_ref[hr, :] + _dot(p[hr, :], jnp.concatenate(vs, axis=0))

    is_last = g == pl.num_programs(1) - 1
    slots = knew_ref.shape[-1]
    r_i = lax.broadcasted_iota(jnp.int32, (n_maps * rows, slots), 0) % rows
    new_mask = (lax.broadcasted_iota(jnp.int32, (n_maps * rows, slots), 1) <= r_i) & is_last
    attend([(r, ()) for r in k_refs] + [(knew_ref, (0,))], [(r, ()) for r in v_refs] + [(vnew_ref, (0,))],
           new_mask)

    @pl.when(is_last)
    def _():
        lam = _diff_lambda(lq1_ref, lk1_ref, lq2_ref, lk2_ref, lam_init)
        o = acc_ref[...] / l_ref[...]
        for hh in range(heads):
            o1 = o[2 * hh * rows:(2 * hh + 1) * rows, :]
            o2 = o[(2 * hh + 1) * rows:(2 * hh + 2) * rows, :]
            o_ref[0, :, hh * vdim:(hh + 1) * vdim] = _diff_finish(o1, o2, lam, gsub_ref[...], lam_init)


def _diff_paged(pt, q8, kt_new, v_new, cache_kt, cache_v, w, *, layer):
    ns, rows, qk_w = q8.shape
    dims = w["dims"]
    heads, dh, vdim = dims["heads"], dims["dh"], dims["vdim"]
    slots = cache_kt.shape[3]
    n_pages = pt.shape[0] // ns
    ppstep = min(PAGES_PER_STEP, n_pages)
    ng = n_pages // ppstep
    per_seq = lambda b, g, pt: (b, 0, 0)
    cm = lambda nd: (lambda b, g, pt: (0,) * nd)
    small = [w["lq1"], w["lk1"], w["lq2"], w["lk2"], w["g_sub"]]
    in_specs = ([pl.BlockSpec((1, rows, qk_w), per_seq),
                 pl.BlockSpec((1, qk_w, slots), per_seq), pl.BlockSpec((1, slots * heads, vdim), per_seq)]
                + [pl.BlockSpec(c.shape, cm(c.ndim), pipeline_mode=pl.Buffered(1)) for c in small]
                + _page_specs((None, None, qk_w, slots), layer, n_pages, ppstep)
                + _page_specs((None, None, slots * heads, vdim), layer, n_pages, ppstep))
    kern = functools.partial(_diff_paged_kernel, ppstep=ppstep, heads=heads, dh=dh, vdim=vdim,
                             scale=dh ** -0.5, lam_init=w["lam_init"])
    grid_spec = pltpu.PrefetchScalarGridSpec(
        num_scalar_prefetch=1, grid=(ns, ng), in_specs=in_specs,
        out_specs=pl.BlockSpec((1, rows, heads * vdim), per_seq),
        scratch_shapes=[pltpu.VMEM((2 * heads * rows, qk_w), BF16),
                        pltpu.VMEM((2 * heads * rows, 1), F32), pltpu.VMEM((2 * heads * rows, 1), F32),
                        pltpu.VMEM((2 * heads * rows, vdim), F32)])
    return pl.pallas_call(
        kern, grid_spec=grid_spec, out_shape=jax.ShapeDtypeStruct((ns, rows, heads * vdim), F32),
        compiler_params=_cparams(2), name="diff_paged",
    )(pt, q8, kt_new, v_new, *small, *([cache_kt] * ppstep), *([cache_v] * ppstep))


def _post_kernel(x_ref, oa_ref, ob_ref, wo_ref, gffn_ref, wup_ref, cw_ref, cb_ref, wdn_ref, *rest,
                 sample, d_ff, fc, tiles_per_batch, dec_seq):
    if sample:
        s1_ref, s2_ref, xo_ref, g_ref, gbuf = rest
    else:
        xo_ref, st_ref, gbuf, carry = rest
    tm = x_ref.shape[0]
    half = oa_ref.shape[1]
    pad = SUBLANES
    x1 = x_ref[...] + _dot(oa_ref[...], wo_ref[0:half, :]) + _dot(ob_ref[...], wo_ref[half:, :])
    h = _rms(x1, gffn_ref[...]).astype(BF16)
    acc = jnp.zeros(x1.shape, F32)
    if sample:
        t_idx = lax.broadcasted_iota(jnp.int32, (tm, fc), 0) % dec_seq
        gbuf[0:pad, :] = jnp.zeros((pad, fc), F32)
    else:
        @pl.when((pl.program_id(0) % tiles_per_batch) == 0)
        def _():
            carry[...] = jnp.zeros(carry.shape, F32)
    for c in range(d_ff // fc):
        cols = slice(c * fc, (c + 1) * fc)
        gate = _dot(h, wup_ref[:, c * fc:(c + 1) * fc])
        up = _dot(h, wup_ref[:, d_ff + c * fc:d_ff + (c + 1) * fc])
        if sample:
            g_ref[:, cols] = gate
            gbuf[pad:pad + tm, :] = gate
            p1 = jnp.where(t_idx >= 1, gbuf[pad - 1:pad - 1 + tm, :], s1_ref[:, cols])
            p2 = jnp.where(t_idx >= 2, gbuf[pad - 2:pad - 2 + tm, :], s2_ref[:, cols])
        else:
            gbuf[0:pad, :] = carry[c]
            gbuf[pad:pad + tm, :] = gate
            carry[c] = gate[tm - pad:, :]
            st_ref[0, :, cols] = gate[tm - pad:, :]
            p1 = gbuf[pad - 1:pad - 1 + tm, :]
            p2 = gbuf[pad - 2:pad - 2 + tm, :]
        y = cb_ref[:, cols] + cw_ref[0:1, cols] * p2 + cw_ref[1:2, cols] * p1 + cw_ref[2:3, cols] * gate
        act = (y * jax.nn.sigmoid(y) * up).astype(BF16)
        acc = acc + _dot(act, wdn_ref[cols, :])
    xo_ref[...] = x1 + acc


def _post(x, oa, ob, w_out, fw, *, sample, tm, tiles_per_batch, s1=None, s2=None, dec_seq=1):
    m, d = x.shape
    half = oa.shape[1]
    d_ff = fw["conv_b"].shape[1]
    fc = FFN_CHUNK
    assert d_ff % fc == 0 and fw["conv_w"].shape[0] == 3
    nt = m // tm
    row = lambda i: (i, 0)
    consts = [w_out, fw["g_ffn"], fw["w_up"], fw["conv_w"], fw["conv_b"], fw["w_down"]]
    in_specs = ([pl.BlockSpec((tm, d), row), pl.BlockSpec((tm, half), row), pl.BlockSpec((tm, half), row)]
                + [_const_spec(c.shape) for c in consts])
    args = [x, oa, ob] + consts
    scratch = [pltpu.VMEM((tm + SUBLANES, fc), F32)]
    if sample:
        in_specs += [pl.BlockSpec((tm, d_ff), row), pl.BlockSpec((tm, d_ff), row)]
        args += [s1, s2]
        out_shape = [jax.ShapeDtypeStruct((m, d), F32), jax.ShapeDtypeStruct((m, d_ff), F32)]
        out_specs = [pl.BlockSpec((tm, d), row), pl.BlockSpec((tm, d_ff), row)]
    else:
        nb = nt // tiles_per_batch
        out_shape = [jax.ShapeDtypeStruct((m, d), F32), jax.ShapeDtypeStruct((nb, SUBLANES, d_ff), F32)]
        out_specs = [pl.BlockSpec((tm, d), row),
                     pl.BlockSpec((1, SUBLANES, d_ff), lambda i: (i // tiles_per_batch, 0, 0))]
        scratch.append(pltpu.VMEM((d_ff // fc, SUBLANES, fc), F32))
    kern = functools.partial(_post_kernel, sample=sample, d_ff=d_ff, fc=fc,
                             tiles_per_batch=tiles_per_batch, dec_seq=dec_seq)
    return pl.pallas_call(
        kern, grid=(nt,), in_specs=in_specs, out_specs=out_specs, out_shape=out_shape,
        scratch_shapes=scratch, compiler_params=_cparams(1), name="post_s" if sample else "post_p",
    )(*args)


def _rope_cos_sin(pos, theta, rot):
    half = rot // 2
    inv = jnp.power(theta, -(jnp.arange(half, dtype=F32) * 2.0 / rot))
    ang = pos.astype(F32)[:, None] * inv[None, :]
    return jnp.cos(ang), jnp.sin(ang)


def _tables(pos, heads, rope_d, dh, qk_w):
    cos, sin = _rope_cos_sin(pos, MLA_THETA, rope_d)
    mla_cs = jnp.tile(jnp.concatenate([cos, cos], axis=1), (1, heads))
    mla_sn = jnp.tile(jnp.concatenate([-sin, sin], axis=1), (1, heads))
    rot = dh // 4
    cos, sin = _rope_cos_sin(pos, DIFF_THETA, rot)
    n = pos.shape[0]
    cs = jnp.concatenate([cos, cos, jnp.ones((n, dh - rot), F32)], axis=1)
    sn = jnp.concatenate([-sin, sin, jnp.zeros((n, dh - rot), F32)], axis=1)
    return {"mla_cs": mla_cs, "mla_sn": mla_sn,
            "diff_cs": jnp.tile(cs, (1, qk_w // dh)), "diff_sn": jnp.tile(sn, (1, qk_w // dh))}


def _mix_matrices(w_s, b_s, mb, t):
    mask = jnp.arange(t)[:, None] >= jnp.arange(t)[None, :]
    wt = jnp.where(mask[None], w_s[:, :t, :t], 0.0)
    reps = mb // t
    eye = jnp.eye(reps, dtype=F32)
    wm = jnp.einsum("ab,gts->gatbs", eye, wt).reshape(w_s.shape[0], mb, mb)
    bm = jnp.broadcast_to(jnp.tile(b_s[:, :t], (1, reps))[:, :, None], (w_s.shape[0], mb, LANES))
    return wm.astype(BF16), bm.astype(F32)


def _prep_even(e, p, mb_s, t_s):
    w_in = p["w_in_even"][e]
    g_v = p["gmlp_g_v"][e]
    w_s, b_s = p["gmlp_w_s"][e], p["gmlp_b_s"][e]
    w_uq, w_ukv = p["mla_w_uq"][e], p["mla_w_ukv"][e]
    q_rank, heads, hd = w_uq.shape
    kv_rank = w_ukv.shape[0]
    rope_d = p["cache_mla_rope"].shape[-1]
    nope = hd - rope_d
    vdim = w_ukv.shape[2] - nope
    a_w = g_v.shape[0]
    groups = w_s.shape[0]
    assert a_w // groups == LANES and 2 * rope_d == LANES and heads % 2 == 0
    o = 2 * a_w + q_rank + kv_rank
    kpe = w_in[:, o:o + rope_d]
    half = rope_d // 2
    kpe_sw = jnp.concatenate([kpe[:, half:], kpe[:, :half]], axis=1)
    w_in2 = jnp.concatenate([w_in[:, :o], kpe, kpe, kpe_sw, kpe_sw], axis=1).astype(BF16)
    qr = w_uq[:, :, nope:]
    qr_sw = jnp.concatenate([qr[:, :, half:], qr[:, :, :half]], axis=2)
    w_uq2 = jnp.concatenate([w_uq[:, :, :nope].reshape(q_rank, heads * nope),
                             qr.reshape(q_rank, heads * rope_d),
                             qr_sw.reshape(q_rank, heads * rope_d)], axis=1).astype(BF16)
    w_uk = w_ukv[:, :, :nope]
    w_uv = w_ukv[:, :, nope:]
    w_ukv2 = jnp.concatenate([w_uk.reshape(kv_rank, heads * nope),
                              w_uv.reshape(kv_rank, heads * vdim)], axis=1).astype(BF16)
    hid = jnp.concatenate([jnp.repeat(jnp.arange(heads), nope), jnp.repeat(jnp.arange(heads), rope_d)])
    e_q = (hid[:, None] == hid[None, :]).astype(BF16)
    g_qn, g_kn = p["mla_g_qn"][e], p["mla_g_kn"][e]
    wmix_p, bmix_p = _mix_matrices(w_s, b_s, GMLP_CHUNK, GMLP_CHUNK)
    wmix_s, bmix_s = _mix_matrices(w_s, b_s, mb_s, t_s)
    return {
        "dims": dict(groups=groups, heads=heads, nope=nope, rope_d=rope_d, vdim=vdim, a_w=a_w,
                     q_rank=q_rank, kv_rank=kv_rank),
        "w_in": w_in2, "g_v": g_v[None, :], "wmix_p": wmix_p, "bmix_p": bmix_p,
        "wmix_s": wmix_s, "bmix_s": bmix_s,
        "g_cq": p["mla_g_cq"][e][None, :], "w_uq": w_uq2, "g_ckv": p["mla_g_ckv"][e][None, :],
        "w_ukv": w_ukv2,
        "g_qn": jnp.concatenate([jnp.tile(g_qn[:nope], heads), jnp.tile(g_qn[nope:], heads)])[None, :],
        "g_kn_n": jnp.tile(g_kn[:nope], heads)[None, :], "g_kn_r": jnp.tile(g_kn[nope:], heads)[None, :],
        "e_q": e_q, "e_k": e_q[:heads * nope, :],
        "w_uk_t": jnp.transpose(w_uk, (1, 2, 0)).reshape(heads * nope, kv_rank).astype(BF16),
        "w_uv": jnp.transpose(w_uv, (1, 0, 2)).astype(BF16),
        "g_kn_n_col": jnp.broadcast_to(g_kn[:nope, None], (nope, LANES)),
        "g_kn_r_col": jnp.broadcast_to(g_kn[nope:, None], (rope_d, LANES)),
        "w_out": p["w_out_even"][e].astype(BF16),
    }


def _prep_odd(o, p):
    w_in = p["w_in_odd"][o]
    c_w = p["conv_w"].shape[2]
    _, _, _, heads, _, dh = p["cache_diff_k"].shape
    vdim = p["cache_diff_v"].shape[-1]
    qk_w = heads * 2 * dh
    assert 2 * dh == LANES and vdim == LANES
    layer = 2 * o + 1
    return {
        "dims": dict(heads=heads, dh=dh, vdim=vdim, c_w=c_w, qk_w=qk_w, v_w=heads * vdim),
        "w_in": w_in.astype(BF16),
        "g_q": jnp.tile(p["diff_g_q"][o], qk_w // dh)[None, :],
        "g_k": jnp.tile(p["diff_g_k"][o], qk_w // dh)[None, :],
        "conv_w": p["conv_w"][o], "conv_b": p["conv_b"][o][None, :],
        "ln_g": p["conv_ln_g"][o][None, :], "ln_b": p["conv_ln_b"][o][None, :],
        "lq1": p["diff_lq1"][o][None, :], "lk1": p["diff_lk1"][o][None, :],
        "lq2": p["diff_lq2"][o][None, :], "lk2": p["diff_lk2"][o][None, :],
        "g_sub": p["diff_g_sub"][o][None, :],
        "lam_init": 0.8 - 0.6 * math.exp(-0.3 * layer),
        "w_out": p["w_out_odd"][o].astype(BF16),
    }


def _prep_ffn(l, p):
    return {"g_ffn": p["g_ffn"][l][None, :], "w_up": p["ffn_w_up"][l].astype(BF16),
            "conv_w": p["ffn_conv_w"][l], "conv_b": p["ffn_conv_b"][l][None, :],
            "w_down": p["ffn_w_down"][l].astype(BF16)}


def _pad_rows(a, rows):
    return jnp.pad(a, ((0, 0), (0, rows - a.shape[1]), (0, 0)))


def kernel(x_prompt, x_sample, cache_mla_latent, cache_mla_rope, cache_diff_k, cache_diff_v, state_conv, state_ffn, page_table, g_mix, g_ffn, w_in_even, w_out_even, gmlp_g_v, gmlp_w_s, gmlp_b_s, mla_g_cq, mla_w_uq, mla_g_ckv, mla_w_ukv, mla_g_qn, mla_g_kn, w_in_odd, w_out_odd, conv_w, conv_b, conv_ln_g, conv_ln_b, diff_g_q, diff_g_k, diff_lq1, diff_lk1, diff_lq2, diff_lk2, diff_g_sub, ffn_w_up, ffn_conv_w, ffn_conv_b, ffn_w_down):
    p = dict(cache_mla_rope=cache_mla_rope, cache_diff_k=cache_diff_k, cache_diff_v=cache_diff_v,
             g_ffn=g_ffn, w_in_even=w_in_even, w_out_even=w_out_even, gmlp_g_v=gmlp_g_v,
             gmlp_w_s=gmlp_w_s, gmlp_b_s=gmlp_b_s, mla_g_cq=mla_g_cq, mla_w_uq=mla_w_uq,
             mla_g_ckv=mla_g_ckv, mla_w_ukv=mla_w_ukv, mla_g_qn=mla_g_qn, mla_g_kn=mla_g_kn,
             w_in_odd=w_in_odd, w_out_odd=w_out_odd, conv_w=conv_w, conv_b=conv_b,
             conv_ln_g=conv_ln_g, conv_ln_b=conv_ln_b, diff_g_q=diff_g_q, diff_g_k=diff_g_k,
             diff_lq1=diff_lq1, diff_lk1=diff_lk1, diff_lq2=diff_lq2, diff_lk2=diff_lk2,
             diff_g_sub=diff_g_sub, ffn_w_up=ffn_w_up, ffn_conv_w=ffn_conv_w, ffn_conv_b=ffn_conv_b,
             ffn_w_down=ffn_w_down)
    batch, seq, d = x_prompt.shape
    ns, t_s, _ = x_sample.shape
    depth = g_mix.shape[0]
    n_pages, slots = page_table.shape[1], cache_mla_latent.shape[2]
    past = n_pages * slots
    mp, ms = batch * seq, ns * t_s
    tm_p = min(TOKEN_TILE, seq)
    tm_s = ms
    mb_s = min(GMLP_CHUNK, ms)
    q_rows = SUBLANES
    n_conv_state = state_conv.shape[2]
    assert seq % tm_p == 0 and tm_p % GMLP_CHUNK == 0 and ms % mb_s == 0 and mb_s % t_s == 0
    assert 2 <= t_s <= q_rows and t_s <= slots and t_s <= n_conv_state and seq >= 32 and past % GMLP_CHUNK == 0
    tpb = seq // tm_p

    heads_d, dh = cache_diff_k.shape[3], cache_diff_k.shape[5]
    qk_w = heads_d * 2 * dh
    heads_m, rope_d = mla_w_uq.shape[2], cache_mla_rope.shape[-1]
    tabs_p = _tables(jnp.arange(seq, dtype=jnp.int32), heads_m, rope_d, dh, qk_w)
    tabs_s = _tables(jnp.tile(past + jnp.arange(t_s, dtype=jnp.int32), ns), heads_m, rope_d, dh, qk_w)

    cache_rope_t = jnp.transpose(cache_mla_rope, (0, 1, 3, 2))
    n_pool = cache_diff_k.shape[1]
    cache_kt = jnp.transpose(cache_diff_k, (0, 1, 3, 4, 5, 2)).reshape(-1, n_pool, qk_w, slots)
    cache_v = cache_diff_v.reshape(-1, n_pool, slots * heads_d, cache_diff_v.shape[-1])
    pt = page_table.reshape(-1)

    xp = x_prompt.reshape(mp, d)
    xs = x_sample.reshape(ms, d)
    outs = {k: [] for k in ("gv_s", "lat_p", "rope_p", "lat_s", "rope_s", "conv_p", "conv_s",
                            "dk_p", "dv_p", "dk_s", "dv_s", "ffn_p", "ffn_s")}
    for l in range(depth):
        fw = _prep_ffn(l, p)
        if l % 2 == 0:
            e = l // 2
            w = _prep_even(e, p, mb_s, t_s)
            w["g_mix"] = g_mix[l][None, :]
            dims = w["dims"]
            oa_p, lat_p, rope_p, q_p, k_p, v_p = _even_pre(xp, tabs_p, w, sample=False, tm=tm_p, pos_rows=seq)
            ob_p = _mla_flash(q_p, k_p, v_p, batch=batch, seq=seq, vdim=dims["vdim"],
                              scale=(dims["nope"] + dims["rope_d"]) ** -0.5)
            oa_s, gv_s, lat_s, rope_s, q_s = _even_pre(xs, tabs_s, w, sample=True, tm=tm_s, pos_rows=ms)
            q8 = _pad_rows(q_s.reshape(ns, t_s, -1), q_rows)
            c_new = _pad_rows(lat_s.reshape(ns, t_s, -1), slots)
            rt_new = jnp.transpose(_pad_rows(rope_s.reshape(ns, t_s, -1), slots), (0, 2, 1))
            o8 = _mla_paged(pt, q8, c_new, rt_new, cache_mla_latent, cache_rope_t, w, layer=e)
            ob_s = o8[:, :t_s].reshape(ms, -1).astype(BF16)
            outs["gv_s"].append(gv_s.reshape(ns, t_s, -1))
            outs["lat_p"].append(lat_p.reshape(batch, seq, -1))
            outs["rope_p"].append(rope_p.reshape(batch, seq, -1))
            outs["lat_s"].append(lat_s.reshape(ns, t_s, -1))
            outs["rope_s"].append(rope_s.reshape(ns, t_s, -1))
        else:
            o = l // 2
            w = _prep_odd(o, p)
            w["g_mix"] = g_mix[l][None, :]
            dims = w["dims"]
            oa_p, q_p, kf_p, kb_p, vf_p, vb_p, st_p = _odd_pre(xp, tabs_p, w, sample=False, tm=tm_p,
                                                               pos_rows=seq, tiles_per_batch=tpb)
            ob_p = _diff_flash(q_p, kb_p, vb_p, w, batch=batch, seq=seq)
            xc_s, q_s, k_s, v_s = _odd_pre(xs, tabs_s, w, sample=True, tm=tm_s, pos_rows=ms, tiles_per_batch=1)
            state_t = jnp.transpose(state_conv[o], (1, 0, 2))
            xc_t = jnp.transpose(xc_s.reshape(ns, t_s, -1), (1, 0, 2))
            oa_s = jnp.transpose(_conv_sample(state_t, xc_t, w), (1, 0, 2)).reshape(ms, -1)
            q8 = _pad_rows(q_s.reshape(ns, t_s, -1), q_rows)
            kt_new = jnp.transpose(_pad_rows(k_s.reshape(ns, t_s, -1), slots), (0, 2, 1))
            v_new = _pad_rows(v_s.reshape(ns, t_s * heads_d, -1), slots * heads_d)
            o8 = _diff_paged(pt, q8, kt_new, v_new, cache_kt, cache_v, w, layer=o)
            ob_s = o8[:, :t_s].reshape(ms, -1).astype(BF16)
            outs["conv_p"].append(st_p[:, st_p.shape[1] - n_conv_state:, :])
            outs["conv_s"].append(jnp.concatenate([state_conv[o][:, t_s:], xc_s.reshape(ns, t_s, -1)], axis=1))
            outs["dk_p"].append(kf_p.reshape(batch, seq, heads_d, 2, dh))
            outs["dv_p"].append(vf_p.reshape(batch, seq, heads_d, -1))
            outs["dk_s"].append(k_s.reshape(ns, t_s, heads_d, 2, dh))
            outs["dv_s"].append(v_s.reshape(ns, t_s, heads_d, -1))
        xp, ffn_st_p = _post(xp, oa_p, ob_p, w["w_out"], fw, sample=False, tm=tm_p, tiles_per_batch=tpb)
        st = state_ffn[l]
        zero = jnp.zeros((ns, t_s - 1, st.shape[-1]), F32)
        s1 = jnp.concatenate([st[:, 1:2], zero], axis=1).reshape(ms, -1)
        s2 = jnp.concatenate([st, zero[:, 1:]], axis=1).reshape(ms, -1)
        xs, g_s = _post(xs, oa_s, ob_s, w["w_out"], fw, sample=True, tm=tm_s, tiles_per_batch=1,
                        s1=s1, s2=s2, dec_seq=t_s)
        outs["ffn_p"].append(ffn_st_p[:, SUBLANES - 2:, :])
        outs["ffn_s"].append(g_s.reshape(ns, t_s, -1)[:, t_s - 2:, :])
    stk = lambda k: jnp.stack(outs[k])
    return (xp.reshape(batch, seq, d), xs.reshape(ns, t_s, d), stk("gv_s"), stk("lat_p"), stk("rope_p"),
            stk("lat_s"), stk("rope_s"), stk("conv_p"), stk("conv_s"), stk("dk_p"), stk("dv_p"),
            stk("dk_s"), stk("dv_s"), stk("ffn_p"), stk("ffn_s"))
```

```python
import functools
import math

import jax
import jax.numpy as jnp
from jax import lax
from jax.experimental import pallas as pl
from jax.experimental.pallas import tpu as pltpu

F32 = jnp.float32
BF16 = jnp.bfloat16

EPS = 1e-6
NEG_INF = -1e30
MLA_THETA = 10000.0
DIFF_THETA = 500000.0
GMLP_CHUNK = 128
LANES = 128
SUBLANES = 8
VMEM_LIMIT_BYTES = 56 * 1024 * 1024
LOG2_E = math.log2(math.e)
TOKEN_TILE = 512
ATTN_TILE = 256
FFN_CHUNK = 256
PAGES_PER_STEP = 16


def _cparams(n_axes):
    return pltpu.CompilerParams(dimension_semantics=("arbitrary",) * n_axes,
                                vmem_limit_bytes=VMEM_LIMIT_BYTES)


def _const_spec(shape):
    zeros = (0,) * len(shape)
    return pl.BlockSpec(shape, lambda *_: zeros, pipeline_mode=pl.Buffered(1))


def _dot(a, b):
    return jnp.dot(a, b, preferred_element_type=F32)


def _dot_nt(a, b):
    return lax.dot_general(a, b, (((1,), (1,)), ((), ())), preferred_element_type=F32)


def _rms(x, g):
    ms = jnp.mean(x * x, axis=-1, keepdims=True)
    return x * lax.rsqrt(ms + EPS) * g


def _dot_f32_by_indicator(x, e):
    hi = x.astype(BF16)
    lo = (x - hi.astype(F32)).astype(BF16)
    return _dot(hi, e) + _dot(lo, e)


def _even_pre_kernel(x_ref, gmix_ref, win_ref, gv_ref, wmix_ref, bmix_ref, gcq_ref, wuq_ref,
                     gckv_ref, wukv_ref, gqn_ref, gknn_ref, gknr_ref, cs_ref, sn_ref, eq_ref,
                     ek_ref, *out_refs, sample, groups, heads, nope, rope_d, q_rank, kv_rank):
    if sample:
        oa_ref, vout_ref, lat_ref, rope_ref, q_ref = out_refs
    else:
        oa_ref, lat_ref, rope_ref, q_ref, k_ref, v_ref = out_refs
    tm = x_ref.shape[0]
    a_w = gv_ref.shape[1]
    gw = a_w // groups
    mb = wmix_ref.shape[1]
    hd = nope + rope_d
    n_nope = heads * nope
    n_rope = heads * rope_d

    x = x_ref[...]
    h = _rms(x, gmix_ref[...]).astype(BF16)
    z = _dot(h, win_ref[...])

    u = jax.nn.gelu(z[:, :a_w])
    v = jax.nn.gelu(z[:, a_w:2 * a_w])
    parts = []
    for g in range(groups):
        vg = v[:, g * gw:(g + 1) * gw]
        parts.append(vg * lax.rsqrt(jnp.mean(vg * vg, axis=-1, keepdims=True) + EPS))
    vn = jnp.concatenate(parts, axis=-1) * gv_ref[...]
    if sample:
        vout_ref[...] = vn
    for c in range(tm // mb):
        rows = slice(c * mb, (c + 1) * mb)
        for g in range(groups):
            cols = slice(g * gw, (g + 1) * gw)
            mix = _dot(wmix_ref[g], vn[rows, cols].astype(BF16)) + bmix_ref[g]
            oa_ref[rows, cols] = (u[rows, cols] * mix).astype(oa_ref.dtype)

    o = 2 * a_w
    r = _rms(z[:, o:o + q_rank], gcq_ref[...]).astype(BF16)
    qall = _dot(r, wuq_ref[...])
    cs = cs_ref[...]
    sn = sn_ref[...]
    q_nope = qall[:, :n_nope]
    q_rope = qall[:, n_nope:n_nope + n_rope] * cs + qall[:, n_nope + n_rope:] * sn
    qq = jnp.concatenate([q_nope, q_rope], axis=-1)
    ss = _dot_f32_by_indicator(qq * qq, eq_ref[...])
    qq = qq * lax.rsqrt(ss * (1.0 / hd) + EPS) * gqn_ref[...]

    o += q_rank
    c = _rms(z[:, o:o + kv_rank], gckv_ref[...])
    lat_ref[...] = c
    o += kv_rank
    kpe2 = z[:, o:o + 2 * rope_d] * cs[:, :2 * rope_d] + z[:, o + 2 * rope_d:o + 4 * rope_d] * sn[:, :2 * rope_d]
    rope_ref[...] = kpe2[:, :rope_d]

    if sample:
        q_ref[...] = qq
        return

    pad = jnp.zeros((tm, q_ref.shape[2] - hd), q_ref.dtype)
    for hh in range(heads):
        q_ref[hh, :, 0:nope] = qq[:, hh * nope:(hh + 1) * nope].astype(q_ref.dtype)
        q_ref[hh, :, nope:hd] = qq[:, n_nope + hh * rope_d:n_nope + (hh + 1) * rope_d].astype(q_ref.dtype)
        if pad.shape[1]:
            q_ref[hh, :, hd:] = pad

    kv = _dot(c.astype(BF16), wukv_ref[...])
    kn = kv[:, :n_nope]
    ssr = jnp.sum(kpe2[:, :rope_d] * kpe2[:, :rope_d], axis=-1, keepdims=True)
    ssk = _dot_f32_by_indicator(kn * kn, ek_ref[...]) + ssr
    inv = lax.rsqrt(ssk * (1.0 / hd) + EPS)
    kn = kn * inv[:, :n_nope] * gknn_ref[...]
    kr = jnp.concatenate([kpe2] * (heads // 2), axis=-1) * inv[:, n_nope:] * gknr_ref[...]
    for hh in range(heads):
        k_ref[hh, :, 0:nope] = kn[:, hh * nope:(hh + 1) * nope].astype(k_ref.dtype)
        k_ref[hh, :, nope:hd] = kr[:, hh * rope_d:(hh + 1) * rope_d].astype(k_ref.dtype)
        if pad.shape[1]:
            k_ref[hh, :, hd:] = pad
    v_ref[...] = kv[:, n_nope:].astype(v_ref.dtype)


def _even_pre(x, tabs, w, *, sample, tm, pos_rows):
    m, d = x.shape
    dims = w["dims"]
    heads, nope, rope_d = dims["heads"], dims["nope"], dims["rope_d"]
    a_w, vdim = dims["a_w"], dims["vdim"]
    hd = nope + rope_d
    n_cat = heads * hd
    hdp = -(-hd // LANES) * LANES
    nt = m // tm
    pos_blocks = pos_rows // tm
    row = lambda i: (i, 0)
    pos = lambda i: (i % pos_blocks, 0)
    consts = [w["g_mix"], w["w_in"], w["g_v"], w["wmix_s" if sample else "wmix_p"],
              w["bmix_s" if sample else "bmix_p"], w["g_cq"], w["w_uq"], w["g_ckv"], w["w_ukv"],
              w["g_qn"], w["g_kn_n"], w["g_kn_r"]]
    in_specs = ([pl.BlockSpec((tm, d), row)] + [_const_spec(c.shape) for c in consts]
                + [pl.BlockSpec((tm, heads * rope_d), pos), pl.BlockSpec((tm, heads * rope_d), pos),
                   _const_spec(w["e_q"].shape), _const_spec(w["e_k"].shape)])
    if sample:
        out_shape = [jax.ShapeDtypeStruct((m, a_w), BF16), jax.ShapeDtypeStruct((m, a_w), F32),
                     jax.ShapeDtypeStruct((m, dims["kv_rank"]), F32), jax.ShapeDtypeStruct((m, rope_d), F32),
                     jax.ShapeDtypeStruct((m, n_cat), F32)]
        out_specs = [pl.BlockSpec((tm, a_w), row), pl.BlockSpec((tm, a_w), row),
                     pl.BlockSpec((tm, dims["kv_rank"]), row), pl.BlockSpec((tm, rope_d), row),
                     pl.BlockSpec((tm, n_cat), row)]
    else:
        out_shape = [jax.ShapeDtypeStruct((m, a_w), BF16),
                     jax.ShapeDtypeStruct((m, dims["kv_rank"]), F32), jax.ShapeDtypeStruct((m, rope_d), F32),
                     jax.ShapeDtypeStruct((heads, m, hdp), BF16), jax.ShapeDtypeStruct((heads, m, hdp), BF16),
                     jax.ShapeDtypeStruct((m, heads * vdim), BF16)]
        out_specs = [pl.BlockSpec((tm, a_w), row),
                     pl.BlockSpec((tm, dims["kv_rank"]), row), pl.BlockSpec((tm, rope_d), row),
                     pl.BlockSpec((heads, tm, hdp), lambda i: (0, i, 0)),
                     pl.BlockSpec((heads, tm, hdp), lambda i: (0, i, 0)),
                     pl.BlockSpec((tm, heads * vdim), row)]
    kern = functools.partial(_even_pre_kernel, sample=sample, groups=dims["groups"], heads=heads,
                             nope=nope, rope_d=rope_d, q_rank=dims["q_rank"], kv_rank=dims["kv_rank"])
    return pl.pallas_call(
        kern, grid=(nt,), in_specs=in_specs, out_specs=out_specs, out_shape=out_shape,
        compiler_params=_cparams(1), name="even_pre_s" if sample else "even_pre_p",
    )(x, *consts, tabs["mla_cs"], tabs["mla_sn"], w["e_q"], w["e_k"])


def _group_norm_rope(q, g, cs, sn, dh):
    tm, n = q.shape
    lane = lax.broadcasted_iota(jnp.int32, (tm, LANES), 1)
    per_vreg = LANES // dh
    cols = []
    for c in range(n // LANES):
        qc = q[:, c * LANES:(c + 1) * LANES]
        sq = qc * qc
        inv = jnp.zeros_like(qc)
        for s in range(per_vreg):
            sel = (lane >= s * dh) & (lane < (s + 1) * dh)
            ms = jnp.sum(jnp.where(sel, sq, 0.0), axis=-1, keepdims=True) * (1.0 / dh)
            inv = jnp.where(sel, lax.rsqrt(ms + EPS), inv)
        cols.append(qc * inv)
    qn = jnp.concatenate(cols, axis=-1) * g
    half = dh // 8
    lane_d = lax.broadcasted_iota(jnp.int32, (tm, n), 1) % dh
    partner = jnp.where(lane_d < half, pltpu.roll(qn, n - half, 1), pltpu.roll(qn, half, 1))
    return qn * cs + partner * sn


def _odd_pre_kernel(x_ref, gmix_ref, win_ref, gq_ref, gk_ref, cs_ref, sn_ref, *rest,
                    sample, c_w, qk_w, dh, tiles_per_batch):
    if sample:
        xc_ref, q_ref, k_ref, v_ref = rest
    else:
        (cw_ref, cb_ref, lng_ref, lnb_ref,
         oc_ref, q_ref, kf_ref, kb_ref, vf_ref, vb_ref, st_ref, xbuf) = rest
    tm = x_ref.shape[0]
    x = x_ref[...]
    h = _rms(x, gmix_ref[...]).astype(BF16)
    z = _dot(h, win_ref[...])
    xc = z[:, :c_w] * jax.nn.sigmoid(z[:, c_w:2 * c_w])
    o = 2 * c_w
    cs = cs_ref[...]
    sn = sn_ref[...]
    q = _group_norm_rope(z[:, o:o + qk_w], gq_ref[...], cs, sn, dh)
    k = _group_norm_rope(z[:, o + qk_w:o + 2 * qk_w], gk_ref[...], cs, sn, dh)
    v = z[:, o + 2 * qk_w:]
    if sample:
        xc_ref[...] = xc
        q_ref[...] = q
        k_ref[...] = k
        v_ref[...] = v
        return
    q_ref[...] = q.astype(q_ref.dtype)
    kf_ref[...] = k
    kb_ref[...] = k.astype(kb_ref.dtype)
    vf_ref[...] = v
    vb_ref[...] = v.astype(vb_ref.dtype)

    n_taps = cw_ref.shape[0]
    halo = xbuf.shape[0] - tm
    first = (pl.program_id(0) % tiles_per_batch) == 0

    @pl.when(first)
    def _():
        xbuf[0:halo, :] = jnp.zeros((halo, c_w), F32)

    @pl.when(jnp.logical_not(first))
    def _():
        xbuf[0:halo, :] = xbuf[tm:tm + halo, :]

    xbuf[halo:halo + tm, :] = xc
    st_ref[0] = xc[tm - halo:, :]
    y = jnp.zeros((tm, c_w), F32) + cb_ref[...]
    base = halo - (n_taps - 1)
    for t in range(n_taps):
        y = y + cw_ref[t:t + 1, :] * xbuf[base + t:base + t + tm, :]
    mu = jnp.mean(y, axis=-1, keepdims=True)
    dlt = y - mu
    yn = dlt * lax.rsqrt(jnp.mean(dlt * dlt, axis=-1, keepdims=True) + EPS) * lng_ref[...] + lnb_ref[...]
    oc_ref[...] = (yn * jax.nn.sigmoid(yn)).astype(oc_ref.dtype)


def _odd_pre(x, tabs, w, *, sample, tm, pos_rows, tiles_per_batch):
    m, d = x.shape
    dims = w["dims"]
    c_w, qk_w, v_w, dh = dims["c_w"], dims["qk_w"], dims["v_w"], dims["dh"]
    nt = m // tm
    pos_blocks = pos_rows // tm
    row = lambda i: (i, 0)
    pos = lambda i: (i % pos_blocks, 0)
    consts = [w["g_mix"], w["w_in"], w["g_q"], w["g_k"]]
    in_specs = ([pl.BlockSpec((tm, d), row)] + [_const_spec(c.shape) for c in consts]
                + [pl.BlockSpec((tm, qk_w), pos), pl.BlockSpec((tm, qk_w), pos)])
    args = [x] + consts + [tabs["diff_cs"], tabs["diff_sn"]]
    scratch = []
    if sample:
        out_shape = [jax.ShapeDtypeStruct((m, c_w), F32), jax.ShapeDtypeStruct((m, qk_w), F32),
                     jax.ShapeDtypeStruct((m, qk_w), F32), jax.ShapeDtypeStruct((m, v_w), F32)]
        out_specs = [pl.BlockSpec((tm, c_w), row), pl.BlockSpec((tm, qk_w), row),
                     pl.BlockSpec((tm, qk_w), row), pl.BlockSpec((tm, v_w), row)]
    else:
        halo = 32
        conv = [w["conv_w"], w["conv_b"], w["ln_g"], w["ln_b"]]
        in_specs += [_const_spec(c.shape) for c in conv]
        args += conv
        nb = nt // tiles_per_batch
        out_shape = [jax.ShapeDtypeStruct((m, c_w), BF16), jax.ShapeDtypeStruct((m, qk_w), BF16),
                     jax.ShapeDtypeStruct((m, qk_w), F32), jax.ShapeDtypeStruct((m, qk_w), BF16),
                     jax.ShapeDtypeStruct((m, v_w), F32), jax.ShapeDtypeStruct((m, v_w), BF16),
                     jax.ShapeDtypeStruct((nb, halo, c_w), F32)]
        out_specs = [pl.BlockSpec((tm, c_w), row), pl.BlockSpec((tm, qk_w), row),
                     pl.BlockSpec((tm, qk_w), row), pl.BlockSpec((tm, qk_w), row),
                     pl.BlockSpec((tm, v_w), row), pl.BlockSpec((tm, v_w), row),
                     pl.BlockSpec((1, halo, c_w), lambda i: (i // tiles_per_batch, 0, 0))]
        scratch = [pltpu.VMEM((tm + halo, c_w), F32)]
    kern = functools.partial(_odd_pre_kernel, sample=sample, c_w=c_w, qk_w=qk_w, dh=dh,
                             tiles_per_batch=tiles_per_batch)
    return pl.pallas_call(
        kern, grid=(nt,), in_specs=in_specs, out_specs=out_specs, out_shape=out_shape,
        scratch_shapes=scratch, compiler_params=_cparams(1),
        name="odd_pre_s" if sample else "odd_pre_p",
    )(*args)


def _conv_sample_kernel(st_ref, xc_ref, cw_ref, cb_ref, lng_ref, lnb_ref, oc_ref):
    n_state = st_ref.shape[0]
    n_taps = cw_ref.shape[0]
    for t in range(xc_ref.shape[0]):
        y = jnp.zeros(oc_ref.shape[1:], F32) + cb_ref[...]
        for kk in range(n_taps):
            idx = t + kk
            src = st_ref[idx] if idx < n_state else xc_ref[idx - n_state]
            y = y + cw_ref[kk:kk + 1, :] * src
        mu = jnp.mean(y, axis=-1, keepdims=True)
        dlt = y - mu
        yn = dlt * lax.rsqrt(jnp.mean(dlt * dlt, axis=-1, keepdims=True) + EPS) * lng_ref[...] + lnb_ref[...]
        oc_ref[t] = (yn * jax.nn.sigmoid(yn)).astype(oc_ref.dtype)


def _conv_sample(state_t, xc_t, w):
    t, n, c = xc_t.shape
    args = [state_t, xc_t, w["conv_w"], w["conv_b"], w["ln_g"], w["ln_b"]]
    return pl.pallas_call(
        _conv_sample_kernel, grid=(1,), in_specs=[_const_spec(a.shape) for a in args],
        out_specs=pl.BlockSpec((t, n, c), lambda i: (0, 0, 0)), out_shape=jax.ShapeDtypeStruct((t, n, c), BF16),
        compiler_params=_cparams(1), name="conv_sample",
    )(*args)


def _softmax_step(s, m, l):
    m_new = jnp.maximum(m, jnp.max(s, axis=-1, keepdims=True))
    alpha = jnp.exp(m - m_new)
    p = jnp.exp(s - m_new)
    return m_new, alpha, p, alpha * l + jnp.sum(p, axis=-1, keepdims=True)


def _flash_t_update(st, idx, vt, m_ref, l_ref, acc_ref, c):
    m_prev = m_ref[idx]
    m_new = jnp.maximum(m_prev, jnp.max(st, axis=0, keepdims=True))
    alpha = jnp.exp2((m_prev - m_new) * c)
    p = jnp.exp2((st - m_new) * c)
    m_ref[idx] = m_new
    l_ref[idx] = alpha * l_ref[idx] + jnp.sum(p, axis=0, keepdims=True)
    acc_ref[idx] = alpha * acc_ref[idx] + _dot(vt, p.astype(BF16))


def _flash_init(m_ref, l_ref, acc_ref):
    m_ref[...] = jnp.full(m_ref.shape, NEG_INF, F32)
    l_ref[...] = jnp.zeros(l_ref.shape, F32)
    acc_ref[...] = jnp.zeros(acc_ref.shape, F32)


def _build_vt(v_ref, vt_ref, heads, vdim):
    tk = vt_ref.shape[-1]
    for hh in range(heads):
        for jb in range(vt_ref.shape[1]):
            blk = v_ref[jb * tk:(jb + 1) * tk, hh * vdim:(hh + 1) * vdim].astype(F32)
            vt_ref[hh, jb] = blk.T.astype(vt_ref.dtype)


def _diag_mask(tk, strip, st):
    key = lax.broadcasted_iota(jnp.int32, (tk, strip), 0)
    qry = lax.broadcasted_iota(jnp.int32, (tk, strip), 1) + st * strip
    return key <= qry


def _mla_flash_kernel(q_ref, k_ref, v_ref, o_ref, qt_ref, vt_ref, m_ref, l_ref, acc_ref, *, heads, vdim, c, strip):
    tq = q_ref.shape[1]
    n_strip = tq // strip
    i = pl.program_id(1)

    @pl.when(i == 0)
    def _():
        _build_vt(v_ref, vt_ref, heads, vdim)

    for hh in range(heads):
        qt_ref[hh] = q_ref[hh].astype(F32).T.astype(qt_ref.dtype)
    _flash_init(m_ref, l_ref, acc_ref)

    def block(j, diag):
        start = pl.multiple_of(j * tq, tq)
        for hh in range(heads):
            kb = k_ref[hh, pl.ds(start, tq), :]
            vt = vt_ref[hh, j]
            for st in range(n_strip):
                s_t = _dot(kb, qt_ref[hh, :, st * strip:(st + 1) * strip])
                if diag:
                    s_t = jnp.where(_diag_mask(tq, strip, st), s_t, NEG_INF)
                _flash_t_update(s_t, hh * n_strip + st, vt, m_ref, l_ref, acc_ref, c)

    def body(j, carry):
        block(j, False)
        return carry

    lax.fori_loop(0, i, body, 0)
    block(i, True)
    for hh in range(heads):
        for st in range(n_strip):
            idx = hh * n_strip + st
            o_t = acc_ref[idx] / l_ref[idx]
            o_ref[st * strip:(st + 1) * strip, hh * vdim:(hh + 1) * vdim] = o_t.T.astype(o_ref.dtype)


def _flash_scratch(n_chain, heads, seq, tq, vdim, strip):
    return [pltpu.VMEM((heads, seq // tq, vdim, tq), BF16),
            pltpu.VMEM((n_chain, 1, strip), F32), pltpu.VMEM((n_chain, 1, strip), F32),
            pltpu.VMEM((n_chain, vdim, strip), F32)]


def _mla_flash(q, k, v, *, batch, seq, vdim, scale):
    heads, m, hd = q.shape
    tq = min(ATTN_TILE, seq)
    nq = seq // tq
    strip = tq
    kern = functools.partial(_mla_flash_kernel, heads=heads, vdim=vdim, c=scale * LOG2_E, strip=strip)
    return pl.pallas_call(
        kern, grid=(batch, nq),
        in_specs=[pl.BlockSpec((heads, tq, hd), lambda b, i: (0, b * nq + i, 0)),
                  pl.BlockSpec((heads, seq, hd), lambda b, i: (0, b, 0)),
                  pl.BlockSpec((seq, heads * vdim), lambda b, i: (b, 0))],
        out_specs=pl.BlockSpec((tq, heads * vdim), lambda b, i: (b * nq + i, 0)),
        out_shape=jax.ShapeDtypeStruct((m, heads * vdim), BF16),
        scratch_shapes=[pltpu.VMEM((heads, hd, tq), BF16)]
                       + _flash_scratch(heads * (tq // strip), heads, seq, tq, vdim, strip),
        compiler_params=_cparams(2), name="mla_flash",
    )(q, k, v)


def _diff_lambda(lq1_ref, lk1_ref, lq2_ref, lk2_ref, lam_init):
    a = jnp.sum(lq1_ref[...] * lk1_ref[...], axis=-1, keepdims=True)
    b = jnp.sum(lq2_ref[...] * lk2_ref[...], axis=-1, keepdims=True)
    return jnp.exp(a) - jnp.exp(b) + lam_init


def _diff_finish(o1, o2, lam, gsub, lam_init):
    o = o1 - lam * o2
    return o * lax.rsqrt(jnp.mean(o * o, axis=-1, keepdims=True) + EPS) * gsub * (1.0 - lam_init)


def _diff_flash_kernel(q_ref, k_ref, v_ref, lq1_ref, lk1_ref, lq2_ref, lk2_ref, gsub_ref, o_ref,
                       qm_ref, vt_ref, m_ref, l_ref, acc_ref, *, heads, dh, vdim, c, lam_init, strip):
    tq = q_ref.shape[0]
    n_strip = tq // strip
    i = pl.program_id(1)

    @pl.when(i == 0)
    def _():
        _build_vt(v_ref, vt_ref, heads, vdim)

    lane = lax.broadcasted_iota(jnp.int32, (tq, 2 * dh), 1)
    _flash_init(m_ref, l_ref, acc_ref)
    for hh in range(heads):
        qh = q_ref[:, hh * 2 * dh:(hh + 1) * 2 * dh]
        zero = jnp.zeros_like(qh)
        qm_ref[2 * hh] = jnp.where(lane < dh, qh, zero)
        qm_ref[2 * hh + 1] = jnp.where(lane < dh, zero, qh)

    def block(j, diag):
        start = pl.multiple_of(j * tq, tq)
        for hh in range(heads):
            kb = k_ref[pl.ds(start, tq), hh * 2 * dh:(hh + 1) * 2 * dh]
            vt = vt_ref[hh, j]
            for comp in range(2):
                for st in range(n_strip):
                    s_t = _dot_nt(kb, qm_ref[2 * hh + comp, st * strip:(st + 1) * strip, :])
                    if diag:
                        s_t = jnp.where(_diag_mask(tq, strip, st), s_t, NEG_INF)
                    _flash_t_update(s_t, (2 * hh + comp) * n_strip + st, vt, m_ref, l_ref, acc_ref, c)

    def body(j, carry):
        block(j, False)
        return carry

    lax.fori_loop(0, i, body, 0)
    block(i, True)
    lam = _diff_lambda(lq1_ref, lk1_ref, lq2_ref, lk2_ref, lam_init)
    for hh in range(heads):
        for st in range(n_strip):
            i1 = (2 * hh) * n_strip + st
            i2 = (2 * hh + 1) * n_strip + st
            o1 = (acc_ref[i1] / l_ref[i1]).T
            o2 = (acc_ref[i2] / l_ref[i2]).T
            out = _diff_finish(o1, o2, lam, gsub_ref[...], lam_init)
            o_ref[st * strip:(st + 1) * strip, hh * vdim:(hh + 1) * vdim] = out.astype(o_ref.dtype)


def _diff_flash(q, k, v, w, *, batch, seq):
    m, qk_w = q.shape
    dims = w["dims"]
    heads, dh, vdim = dims["heads"], dims["dh"], dims["vdim"]
    tq = min(ATTN_TILE, seq)
    nq = seq // tq
    small = [w["lq1"], w["lk1"], w["lq2"], w["lk2"], w["g_sub"]]
    strip = LANES
    kern = functools.partial(_diff_flash_kernel, heads=heads, dh=dh, vdim=vdim, c=dh ** -0.5 * LOG2_E,
                             lam_init=w["lam_init"], strip=strip)
    return pl.pallas_call(
        kern, grid=(batch, nq),
        in_specs=[pl.BlockSpec((tq, qk_w), lambda b, i: (b * nq + i, 0)),
                  pl.BlockSpec((seq, qk_w), lambda b, i: (b, 0)),
                  pl.BlockSpec((seq, heads * vdim), lambda b, i: (b, 0))]
                 + [_const_spec(a.shape) for a in small],
        out_specs=pl.BlockSpec((tq, heads * vdim), lambda b, i: (b * nq + i, 0)),
        out_shape=jax.ShapeDtypeStruct((m, heads * vdim), BF16),
        scratch_shapes=[pltpu.VMEM((2 * heads, tq, 2 * dh), BF16)]
                       + _flash_scratch(2 * heads * (tq // strip), heads, seq, tq, vdim, strip),
        compiler_params=_cparams(2), name="diff_flash",
    )(q, k, v, *small)


def _page_specs(block, layer, n_pages, ppstep):
    def make(i):
        return pl.BlockSpec(block, lambda b, g, pt: (layer, pt[b * n_pages + g * ppstep + i], 0, 0))
    return [make(i) for i in range(ppstep)]


def _mla_paged_kernel(pt_ref, q_ref, cnew_ref, rnew_ref, wukt_ref, wuv_ref, gn_ref, gr_ref, *rest,
                      ppstep, heads, nope, rope_d, scale):
    lat_refs = rest[:ppstep]
    rope_refs = rest[ppstep:2 * ppstep]
    o_ref, lhs_ref, qr_ref, m_ref, l_ref, acc_ref = rest[2 * ppstep:]
    g = pl.program_id(1)
    hd = nope + rope_d
    n_nope = heads * nope
    rows = q_ref.shape[1]

    @pl.when(g == 0)
    def _():
        q = q_ref[0]
        lhs_ref[0:n_nope, :] = wukt_ref[...]
        qa, qr = [], []
        for hh in range(heads):
            qg = (q[:, hh * nope:(hh + 1) * nope] * gn_ref[...]).astype(BF16)
            qa.append(_dot(qg, wukt_ref[hh * nope:(hh + 1) * nope, :]))
            qr.append(q[:, n_nope + hh * rope_d:n_nope + (hh + 1) * rope_d] * gr_ref[...])
        lhs_ref[n_nope:, :] = jnp.concatenate(qa, axis=0).astype(BF16)
        qr_ref[...] = jnp.concatenate(qr, axis=0).astype(BF16)
        _flash_init(m_ref, l_ref, acc_ref)

    def attend(cs, rts, mask):
        cb = [c.astype(BF16) for c in cs]
        n_cache = len(cb) - 1
        groups = [(jnp.concatenate(cb[i:i + 2], axis=0), jnp.concatenate(rts[i:i + 2], axis=1))
                  for i in range(0, n_cache - 1, 2)]
        groups += [(cb[i], rts[i]) for i in range(n_cache - n_cache % 2, len(cb))]
        scores = []
        for c16, rt in groups:
            slots = c16.shape[0]
            big = _dot_nt(lhs_ref[...], c16)
            ssr = jnp.sum(rt * rt, axis=0, keepdims=True)
            inv = []
            for hh in range(heads):
                kt = big[hh * nope:(hh + 1) * nope, :]
                ss = jnp.sum(kt * kt, axis=0, keepdims=True) + ssr
                inv.append(jnp.broadcast_to(lax.rsqrt(ss * (1.0 / hd) + EPS) * scale, (rows, slots)))
            s = big[n_nope:, :] + _dot(qr_ref[...], rt.astype(BF16))
            scores.append(s * jnp.concatenate(inv, axis=0))
        scores[-1] = jnp.where(mask, scores[-1], NEG_INF)
        s = jnp.concatenate(scores, axis=-1)
        m, alpha, p, l = _softmax_step(s, m_ref[...], l_ref[...])
        m_ref[...] = m
        l_ref[...] = l
        acc_ref[...] = alpha * acc_ref[...] + _dot(p.astype(BF16), jnp.concatenate(cb, axis=0))

    is_last = g == pl.num_programs(1) - 1
    slots = cnew_ref.shape[1]
    r_i = lax.broadcasted_iota(jnp.int32, (heads * rows, slots), 0) % rows
    new_mask = (lax.broadcasted_iota(jnp.int32, (heads * rows, slots), 1) <= r_i) & is_last
    attend([r[...] for r in lat_refs] + [cnew_ref[0]], [r[...] for r in rope_refs] + [rnew_ref[0]], new_mask)

    @pl.when(is_last)
    def _():
        vdim = wuv_ref.shape[2]
        ol = (acc_ref[...] / l_ref[...]).astype(BF16)
        for hh in range(heads):
            o_ref[0, :, hh * vdim:(hh + 1) * vdim] = _dot(ol[hh * rows:(hh + 1) * rows, :], wuv_ref[hh])


def _mla_paged(pt, q8, c_new, rt_new, cache_lat, cache_rope_t, w, *, layer):
    ns, rows, n_cat = q8.shape
    dims = w["dims"]
    heads, nope, rope_d, vdim, rank = dims["heads"], dims["nope"], dims["rope_d"], dims["vdim"], dims["kv_rank"]
    slots = cache_lat.shape[2]
    n_pages = pt.shape[0] // ns
    ppstep = min(PAGES_PER_STEP, n_pages)
    ng = n_pages // ppstep
    per_seq = lambda b, g, pt: (b, 0, 0)
    cm = lambda nd: (lambda b, g, pt: (0,) * nd)
    consts = [w["w_uk_t"], w["w_uv"], w["g_kn_n"][:, :nope], w["g_kn_r"][:, :rope_d]]
    in_specs = ([pl.BlockSpec((1, rows, n_cat), per_seq),
                 pl.BlockSpec((1, slots, rank), per_seq), pl.BlockSpec((1, rope_d, slots), per_seq)]
                + [pl.BlockSpec(c.shape, cm(c.ndim), pipeline_mode=pl.Buffered(1)) for c in consts]
                + _page_specs((None, None, slots, rank), layer, n_pages, ppstep)
                + _page_specs((None, None, rope_d, slots), layer, n_pages, ppstep))
    kern = functools.partial(_mla_paged_kernel, ppstep=ppstep, heads=heads, nope=nope, rope_d=rope_d,
                             scale=(nope + rope_d) ** -0.5)
    grid_spec = pltpu.PrefetchScalarGridSpec(
        num_scalar_prefetch=1, grid=(ns, ng), in_specs=in_specs,
        out_specs=pl.BlockSpec((1, rows, heads * vdim), per_seq),
        scratch_shapes=[pltpu.VMEM((heads * (nope + rows), rank), BF16),
                        pltpu.VMEM((heads * rows, rope_d), BF16),
                        pltpu.VMEM((heads * rows, 1), F32), pltpu.VMEM((heads * rows, 1), F32),
                        pltpu.VMEM((heads * rows, rank), F32)])
    return pl.pallas_call(
        kern, grid_spec=grid_spec, out_shape=jax.ShapeDtypeStruct((ns, rows, heads * vdim), F32),
        compiler_params=_cparams(2), name="mla_paged",
    )(pt, q8, c_new, rt_new, *consts, *([cache_lat] * ppstep), *([cache_rope_t] * ppstep))


def _diff_paged_kernel(pt_ref, q_ref, knew_ref, vnew_ref, lq1_ref, lk1_ref, lq2_ref, lk2_ref, gsub_ref,
                       *rest, ppstep, heads, dh, vdim, scale, lam_init):
    k_refs = rest[:ppstep]
    v_refs = rest[ppstep:2 * ppstep]
    o_ref, qbd_ref, m_ref, l_ref, acc_ref = rest[2 * ppstep:]
    g = pl.program_id(1)
    rows = q_ref.shape[1]
    n_maps = 2 * heads
    qk_w = n_maps * dh

    @pl.when(g == 0)
    def _():
        q = q_ref[0]
        lane = lax.broadcasted_iota(jnp.int32, (rows, qk_w), 1)
        zero = jnp.zeros_like(q)
        blocks = [jnp.where((lane >= i * dh) & (lane < (i + 1) * dh), q, zero) for i in range(n_maps)]
        qbd_ref[...] = jnp.concatenate(blocks, axis=0).astype(BF16)
        _flash_init(m_ref, l_ref, acc_ref)

    def attend(kt_refs, vv_refs, mask):
        slots = kt_refs[0][0].shape[-1]
        full = (slice(None), slice(None))
        parts = [_dot(qbd_ref[...], r[sl + full].astype(BF16)) * scale for r, sl in kt_refs]
        parts[-1] = jnp.where(mask, parts[-1], NEG_INF)
        s = jnp.concatenate(parts, axis=-1)
        m, alpha, p, l = _softmax_step(s, m_ref[...], l_ref[...])
        m_ref[...] = m
        l_ref[...] = l
        p = p.astype(BF16)
        for hh in range(heads):
            hr = slice(hh * 2 * rows, (hh + 1) * 2 * rows)
            vs = [r[sl + (pl.ds(hh, slots, stride=heads), slice(None))].astype(BF16) for r, sl in vv_refs]
            acc_ref[hr, :] = alpha[hr, :] * acc_ref[hr, :] + _dot(p[hr, :], jnp.concatenate(vs, axis=0))

    is_last = g == pl.num_programs(1) - 1
    slots = knew_ref.shape[-1]
    r_i = lax.broadcasted_iota(jnp.int32, (n_maps * rows, slots), 0) % rows
    new_mask = (lax.broadcasted_iota(jnp.int32, (n_maps * rows, slots), 1) <= r_i) & is_last
    attend([(r, ()) for r in k_refs] + [(knew_ref, (0,))], [(r, ()) for r in v_refs] + [(vnew_ref, (0,))],
           new_mask)

    @pl.when(is_last)
    def _():
        lam = _diff_lambda(lq1_ref, lk1_ref, lq2_ref, lk2_ref, lam_init)
        o = acc_ref[...] / l_ref[...]
        for hh in range(heads):
            o1 = o[2 * hh * rows:(2 * hh + 1) * rows, :]
            o2 = o[(2 * hh + 1) * rows:(2 * hh + 2) * rows, :]
            o_ref[0, :, hh * vdim:(hh + 1) * vdim] = _diff_finish(o1, o2, lam, gsub_ref[...], lam_init)


def _diff_paged(pt, q8, kt_new, v_new, cache_kt, cache_v, w, *, layer):
    ns, rows, qk_w = q8.shape
    dims = w["dims"]
    heads, dh, vdim = dims["heads"], dims["dh"], dims["vdim"]
    slots = cache_kt.shape[3]
    n_pages = pt.shape[0] // ns
    ppstep = min(PAGES_PER_STEP, n_pages)
    ng = n_pages // ppstep
    per_seq = lambda b, g, pt: (b, 0, 0)
    cm = lambda nd: (lambda b, g, pt: (0,) * nd)
    small = [w["lq1"], w["lk1"], w["lq2"], w["lk2"], w["g_sub"]]
    in_specs = ([pl.BlockSpec((1, rows, qk_w), per_seq),
                 pl.BlockSpec((1, qk_w, slots), per_seq), pl.BlockSpec((1, slots * heads, vdim), per_seq)]
                + [pl.BlockSpec(c.shape, cm(c.ndim), pipeline_mode=pl.Buffered(1)) for c in small]
                + _page_specs((None, None, qk_w, slots), layer, n_pages, ppstep)
                + _page_specs((None, None, slots * heads, vdim), layer, n_pages, ppstep))
    kern = functools.partial(_diff_paged_kernel, ppstep=ppstep, heads=heads, dh=dh, vdim=vdim,
                             scale=dh ** -0.5, lam_init=w["lam_init"])
    grid_spec = pltpu.PrefetchScalarGridSpec(
        num_scalar_prefetch=1, grid=(ns, ng), in_specs=in_specs,
        out_specs=pl.BlockSpec((1, rows, heads * vdim), per_seq),
        scratch_shapes=[pltpu.VMEM((2 * heads * rows, qk_w), BF16),
                        pltpu.VMEM((2 * heads * rows, 1), F32), pltpu.VMEM((2 * heads * rows, 1), F32),
                        pltpu.VMEM((2 * heads * rows, vdim), F32)])
    return pl.pallas_call(
        kern, grid_spec=grid_spec, out_shape=jax.ShapeDtypeStruct((ns, rows, heads * vdim), F32),
        compiler_params=_cparams(2), name="diff_paged",
    )(pt, q8, kt_new, v_new, *small, *([cache_kt] * ppstep), *([cache_v] * ppstep))


def _post_kernel(x_ref, oa_ref, ob_ref, wo_ref, gffn_ref, wup_ref, cw_ref, cb_ref, wdn_ref, *rest,
                 sample, d_ff, fc, tiles_per_batch, dec_seq):
    if sample:
        s1_ref, s2_ref, xo_ref, g_ref, gbuf, act_ref = rest
    else:
        xo_ref, st_ref, gbuf, act_ref, carry = rest
    tm = x_ref.shape[0]
    half = oa_ref.shape[1]
    pad = SUBLANES
    x1 = x_ref[...] + _dot(oa_ref[...], wo_ref[0:half, :]) + _dot(ob_ref[...], wo_ref[half:, :])
    h = _rms(x1, gffn_ref[...]).astype(BF16)
    if sample:
        t_idx = lax.broadcasted_iota(jnp.int32, (tm, fc), 0) % dec_seq
        gbuf[0:pad, :] = jnp.zeros((pad, fc), F32)
    else:
        @pl.when((pl.program_id(0) % tiles_per_batch) == 0)
        def _():
            carry[...] = jnp.zeros(carry.shape, F32)
    for c in range(d_ff // fc):
        cols = slice(c * fc, (c + 1) * fc)
        gate = _dot(h, wup_ref[:, c * fc:(c + 1) * fc])
        up = _dot(h, wup_ref[:, d_ff + c * fc:d_ff + (c + 1) * fc])
        if sample:
            g_ref[:, cols] = gate
            gbuf[pad:pad + tm, :] = gate
            p1 = jnp.where(t_idx >= 1, gbuf[pad - 1:pad - 1 + tm, :], s1_ref[:, cols])
            p2 = jnp.where(t_idx >= 2, gbuf[pad - 2:pad - 2 + tm, :], s2_ref[:, cols])
        else:
            gbuf[0:pad, :] = carry[c]
            gbuf[pad:pad + tm, :] = gate
            carry[c] = gate[tm - pad:, :]
            st_ref[0, :, cols] = gate[tm - pad:, :]
            p1 = gbuf[pad - 1:pad - 1 + tm, :]
            p2 = gbuf[pad - 2:pad - 2 + tm, :]
        y = cb_ref[:, cols] + cw_ref[0:1, cols] * p2 + cw_ref[1:2, cols] * p1 + cw_ref[2:3, cols] * gate
        act_ref[:, cols] = (y * jax.nn.sigmoid(y) * up).astype(act_ref.dtype)
    xo_ref[...] = x1 + _dot(act_ref[...], wdn_ref[...])


def _post(x, oa, ob, w_out, fw, *, sample, tm, tiles_per_batch, s1=None, s2=None, dec_seq=1):
    m, d = x.shape
    half = oa.shape[1]
    d_ff = fw["conv_b"].shape[1]
    fc = FFN_CHUNK
    assert d_ff % fc == 0 and fw["conv_w"].shape[0] == 3
    nt = m // tm
    row = lambda i: (i, 0)
    consts = [w_out, fw["g_ffn"], fw["w_up"], fw["conv_w"], fw["conv_b"], fw["w_down"]]
    in_specs = ([pl.BlockSpec((tm, d), row), pl.BlockSpec((tm, half), row), pl.BlockSpec((tm, half), row)]
                + [_const_spec(c.shape) for c in consts])
    args = [x, oa, ob] + consts
    scratch = [pltpu.VMEM((tm + SUBLANES, fc), F32), pltpu.VMEM((tm, d_ff), BF16)]
    if sample:
        in_specs += [pl.BlockSpec((tm, d_ff), row), pl.BlockSpec((tm, d_ff), row)]
        args += [s1, s2]
        out_shape = [jax.ShapeDtypeStruct((m, d), F32), jax.ShapeDtypeStruct((m, d_ff), F32)]
        out_specs = [pl.BlockSpec((tm, d), row), pl.BlockSpec((tm, d_ff), row)]
    else:
        nb = nt // tiles_per_batch
        out_shape = [jax.ShapeDtypeStruct((m, d), F32), jax.ShapeDtypeStruct((nb, SUBLANES, d_ff), F32)]
        out_specs = [pl.BlockSpec((tm, d), row),
                     pl.BlockSpec((1, SUBLANES, d_ff), lambda i: (i // tiles_per_batch, 0, 0))]
        scratch.append(pltpu.VMEM((d_ff // fc, SUBLANES, fc), F32))
    kern = functools.partial(_post_kernel, sample=sample, d_ff=d_ff, fc=fc,
                             tiles_per_batch=tiles_per_batch, dec_seq=dec_seq)
    return pl.pallas_call(
        kern, grid=(nt,), in_specs=in_specs, out_specs=out_specs, out_shape=out_shape,
        scratch_shapes=scratch, compiler_params=_cparams(1), name="post_s" if sample else "post_p",
    )(*args)


def _rope_cos_sin(pos, theta, rot):
    half = rot // 2
    inv = jnp.power(theta, -(jnp.arange(half, dtype=F32) * 2.0 / rot))
    ang = pos.astype(F32)[:, None] * inv[None, :]
    return jnp.cos(ang), jnp.sin(ang)


def _tables(pos, heads, rope_d, dh, qk_w):
    cos, sin = _rope_cos_sin(pos, MLA_THETA, rope_d)
    mla_cs = jnp.tile(jnp.concatenate([cos, cos], axis=1), (1, heads))
    mla_sn = jnp.tile(jnp.concatenate([-sin, sin], axis=1), (1, heads))
    rot = dh // 4
    cos, sin = _rope_cos_sin(pos, DIFF_THETA, rot)
    n = pos.shape[0]
    cs = jnp.concatenate([cos, cos, jnp.ones((n, dh - rot), F32)], axis=1)
    sn = jnp.concatenate([-sin, sin, jnp.zeros((n, dh - rot), F32)], axis=1)
    return {"mla_cs": mla_cs, "mla_sn": mla_sn,
            "diff_cs": jnp.tile(cs, (1, qk_w // dh)), "diff_sn": jnp.tile(sn, (1, qk_w // dh))}


def _mix_matrices(w_s, b_s, mb, t):
    mask = jnp.arange(t)[:, None] >= jnp.arange(t)[None, :]
    wt = jnp.where(mask[None], w_s[:, :t, :t], 0.0)
    reps = mb // t
    eye = jnp.eye(reps, dtype=F32)
    wm = jnp.einsum("ab,gts->gatbs", eye, wt).reshape(w_s.shape[0], mb, mb)
    bm = jnp.broadcast_to(jnp.tile(b_s[:, :t], (1, reps))[:, :, None], (w_s.shape[0], mb, LANES))
    return wm.astype(BF16), bm.astype(F32)


def _prep_even(e, p, mb_s, t_s):
    w_in = p["w_in_even"][e]
    g_v = p["gmlp_g_v"][e]
    w_s, b_s = p["gmlp_w_s"][e], p["gmlp_b_s"][e]
    w_uq, w_ukv = p["mla_w_uq"][e], p["mla_w_ukv"][e]
    q_rank, heads, hd = w_uq.shape
    kv_rank = w_ukv.shape[0]
    rope_d = p["cache_mla_rope"].shape[-1]
    nope = hd - rope_d
    vdim = w_ukv.shape[2] - nope
    a_w = g_v.shape[0]
    groups = w_s.shape[0]
    assert a_w // groups == LANES and 2 * rope_d == LANES and heads % 2 == 0
    o = 2 * a_w + q_rank + kv_rank
    kpe = w_in[:, o:o + rope_d]
    half = rope_d // 2
    kpe_sw = jnp.concatenate([kpe[:, half:], kpe[:, :half]], axis=1)
    w_in2 = jnp.concatenate([w_in[:, :o], kpe, kpe, kpe_sw, kpe_sw], axis=1).astype(BF16)
    qr = w_uq[:, :, nope:]
    qr_sw = jnp.concatenate([qr[:, :, half:], qr[:, :, :half]], axis=2)
    w_uq2 = jnp.concatenate([w_uq[:, :, :nope].reshape(q_rank, heads * nope),
                             qr.reshape(q_rank, heads * rope_d),
                             qr_sw.reshape(q_rank, heads * rope_d)], axis=1).astype(BF16)
    w_uk = w_ukv[:, :, :nope]
    w_uv = w_ukv[:, :, nope:]
    w_ukv2 = jnp.concatenate([w_uk.reshape(kv_rank, heads * nope),
                              w_uv.reshape(kv_rank, heads * vdim)], axis=1).astype(BF16)
    hid = jnp.concatenate([jnp.repeat(jnp.arange(heads), nope), jnp.repeat(jnp.arange(heads), rope_d)])
    e_q = (hid[:, None] == hid[None, :]).astype(BF16)
    g_qn, g_kn = p["mla_g_qn"][e], p["mla_g_kn"][e]
    wmix_p, bmix_p = _mix_matrices(w_s, b_s, GMLP_CHUNK, GMLP_CHUNK)
    wmix_s, bmix_s = _mix_matrices(w_s, b_s, mb_s, t_s)
    return {
        "dims": dict(groups=groups, heads=heads, nope=nope, rope_d=rope_d, vdim=vdim, a_w=a_w,
                     q_rank=q_rank, kv_rank=kv_rank),
        "w_in": w_in2, "g_v": g_v[None, :], "wmix_p": wmix_p, "bmix_p": bmix_p,
        "wmix_s": wmix_s, "bmix_s": bmix_s,
        "g_cq": p["mla_g_cq"][e][None, :], "w_uq": w_uq2, "g_ckv": p["mla_g_ckv"][e][None, :],
        "w_ukv": w_ukv2,
        "g_qn": jnp.concatenate([jnp.tile(g_qn[:nope], heads), jnp.tile(g_qn[nope:], heads)])[None, :],
        "g_kn_n": jnp.tile(g_kn[:nope], heads)[None, :], "g_kn_r": jnp.tile(g_kn[nope:], heads)[None, :],
        "e_q": e_q, "e_k": e_q[:heads * nope, :],
        "w_uk_t": jnp.transpose(w_uk, (1, 2, 0)).reshape(heads * nope, kv_rank).astype(BF16),
        "w_uv": jnp.transpose(w_uv, (1, 0, 2)).astype(BF16),
        "w_out": p["w_out_even"][e].astype(BF16),
    }


def _prep_odd(o, p):
    w_in = p["w_in_odd"][o]
    c_w = p["conv_w"].shape[2]
    _, _, _, heads, _, dh = p["cache_diff_k"].shape
    vdim = p["cache_diff_v"].shape[-1]
    qk_w = heads * 2 * dh
    assert 2 * dh == LANES and vdim == LANES
    layer = 2 * o + 1
    return {
        "dims": dict(heads=heads, dh=dh, vdim=vdim, c_w=c_w, qk_w=qk_w, v_w=heads * vdim),
        "w_in": w_in.astype(BF16),
        "g_q": jnp.tile(p["diff_g_q"][o], qk_w // dh)[None, :],
        "g_k": jnp.tile(p["diff_g_k"][o], qk_w // dh)[None, :],
        "conv_w": p["conv_w"][o], "conv_b": p["conv_b"][o][None, :],
        "ln_g": p["conv_ln_g"][o][None, :], "ln_b": p["conv_ln_b"][o][None, :],
        "lq1": p["diff_lq1"][o][None, :], "lk1": p["diff_lk1"][o][None, :],
        "lq2": p["diff_lq2"][o][None, :], "lk2": p["diff_lk2"][o][None, :],
        "g_sub": p["diff_g_sub"][o][None, :],
        "lam_init": 0.8 - 0.6 * math.exp(-0.3 * layer),
        "w_out": p["w_out_odd"][o].astype(BF16),
    }


def _prep_ffn(l, p):
    return {"g_ffn": p["g_ffn"][l][None, :], "w_up": p["ffn_w_up"][l].astype(BF16),
            "conv_w": p["ffn_conv_w"][l], "conv_b": p["ffn_conv_b"][l][None, :],
            "w_down": p["ffn_w_down"][l].astype(BF16)}


def _pad_rows(a, rows):
    return jnp.pad(a, ((0, 0), (0, rows - a.shape[1]), (0, 0)))


def kernel(x_prompt, x_sample, cache_mla_latent, cache_mla_rope, cache_diff_k, cache_diff_v, state_conv, state_ffn, page_table, g_mix, g_ffn, w_in_even, w_out_even, gmlp_g_v, gmlp_w_s, gmlp_b_s, mla_g_cq, mla_w_uq, mla_g_ckv, mla_w_ukv, mla_g_qn, mla_g_kn, w_in_odd, w_out_odd, conv_w, conv_b, conv_ln_g, conv_ln_b, diff_g_q, diff_g_k, diff_lq1, diff_lk1, diff_lq2, diff_lk2, diff_g_sub, ffn_w_up, ffn_conv_w, ffn_conv_b, ffn_w_down):
    p = dict(cache_mla_rope=cache_mla_rope, cache_diff_k=cache_diff_k, cache_diff_v=cache_diff_v,
             g_ffn=g_ffn, w_in_even=w_in_even, w_out_even=w_out_even, gmlp_g_v=gmlp_g_v,
             gmlp_w_s=gmlp_w_s, gmlp_b_s=gmlp_b_s, mla_g_cq=mla_g_cq, mla_w_uq=mla_w_uq,
             mla_g_ckv=mla_g_ckv, mla_w_ukv=mla_w_ukv, mla_g_qn=mla_g_qn, mla_g_kn=mla_g_kn,
             w_in_odd=w_in_odd, w_out_odd=w_out_odd, conv_w=conv_w, conv_b=conv_b,
             conv_ln_g=conv_ln_g, conv_ln_b=conv_ln_b, diff_g_q=diff_g_q, diff_g_k=diff_g_k,
             diff_lq1=diff_lq1, diff_lk1=diff_lk1, diff_lq2=diff_lq2, diff_lk2=diff_lk2,
             diff_g_sub=diff_g_sub, ffn_w_up=ffn_w_up, ffn_conv_w=ffn_conv_w, ffn_conv_b=ffn_conv_b,
             ffn_w_down=ffn_w_down)
    batch, seq, d = x_prompt.shape
    ns, t_s, _ = x_sample.shape
    depth = g_mix.shape[0]
    n_pages, slots = page_table.shape[1], cache_mla_latent.shape[2]
    past = n_pages * slots
    mp, ms = batch * seq, ns * t_s
    tm_p = min(TOKEN_TILE, seq)
    tm_s = ms
    mb_s = min(GMLP_CHUNK, ms)
    q_rows = SUBLANES
    n_conv_state = state_conv.shape[2]
    assert seq % tm_p == 0 and tm_p % GMLP_CHUNK == 0 and ms % mb_s == 0 and mb_s % t_s == 0
    assert 2 <= t_s <= q_rows and t_s <= slots and t_s <= n_conv_state and seq >= 32 and past % GMLP_CHUNK == 0
    tpb = seq // tm_p

    heads_d, dh = cache_diff_k.shape[3], cache_diff_k.shape[5]
    qk_w = heads_d * 2 * dh
    heads_m, rope_d = mla_w_uq.shape[2], cache_mla_rope.shape[-1]
    tabs_p = _tables(jnp.arange(seq, dtype=jnp.int32), heads_m, rope_d, dh, qk_w)
    tabs_s = _tables(jnp.tile(past + jnp.arange(t_s, dtype=jnp.int32), ns), heads_m, rope_d, dh, qk_w)

    cache_rope_t = jnp.transpose(cache_mla_rope, (0, 1, 3, 2))
    n_pool = cache_diff_k.shape[1]
    cache_kt = jnp.transpose(cache_diff_k, (0, 1, 3, 4, 5, 2)).reshape(-1, n_pool, qk_w, slots)
    cache_v = cache_diff_v.reshape(-1, n_pool, slots * heads_d, cache_diff_v.shape[-1])
    pt = page_table.reshape(-1)

    xp = x_prompt.reshape(mp, d)
    xs = x_sample.reshape(ms, d)
    outs = {k: [] for k in ("gv_s", "lat_p", "rope_p", "lat_s", "rope_s", "conv_p", "conv_s",
                            "dk_p", "dv_p", "dk_s", "dv_s", "ffn_p", "ffn_s")}
    for l in range(depth):
        fw = _prep_ffn(l, p)
        if l % 2 == 0:
            e = l // 2
            w = _prep_even(e, p, mb_s, t_s)
            w["g_mix"] = g_mix[l][None, :]
            dims = w["dims"]
            oa_p, lat_p, rope_p, q_p, k_p, v_p = _even_pre(xp, tabs_p, w, sample=False, tm=tm_p, pos_rows=seq)
            ob_p = _mla_flash(q_p, k_p, v_p, batch=batch, seq=seq, vdim=dims["vdim"],
                              scale=(dims["nope"] + dims["rope_d"]) ** -0.5)
            oa_s, gv_s, lat_s, rope_s, q_s = _even_pre(xs, tabs_s, w, sample=True, tm=tm_s, pos_rows=ms)
            q8 = _pad_rows(q_s.reshape(ns, t_s, -1), q_rows)
            c_new = _pad_rows(lat_s.reshape(ns, t_s, -1), slots)
            rt_new = jnp.transpose(_pad_rows(rope_s.reshape(ns, t_s, -1), slots), (0, 2, 1))
            o8 = _mla_paged(pt, q8, c_new, rt_new, cache_mla_latent, cache_rope_t, w, layer=e)
            ob_s = o8[:, :t_s].reshape(ms, -1).astype(BF16)
            outs["gv_s"].append(gv_s.reshape(ns, t_s, -1))
            outs["lat_p"].append(lat_p.reshape(batch, seq, -1))
            outs["rope_p"].append(rope_p.reshape(batch, seq, -1))
            outs["lat_s"].append(lat_s.reshape(ns, t_s, -1))
            outs["rope_s"].append(rope_s.reshape(ns, t_s, -1))
        else:
            o = l // 2
            w = _prep_odd(o, p)
            w["g_mix"] = g_mix[l][None, :]
            dims = w["dims"]
            oa_p, q_p, kf_p, kb_p, vf_p, vb_p, st_p = _odd_pre(xp, tabs_p, w, sample=False, tm=tm_p,
                                                               pos_rows=seq, tiles_per_batch=tpb)
            ob_p = _diff_flash(q_p, kb_p, vb_p, w, batch=batch, seq=seq)
            xc_s, q_s, k_s, v_s = _odd_pre(xs, tabs_s, w, sample=True, tm=tm_s, pos_rows=ms, tiles_per_batch=1)
            state_t = jnp.transpose(state_conv[o], (1, 0, 2))
            xc_t = jnp.transpose(xc_s.reshape(ns, t_s, -1), (1, 0, 2))
            oa_s = jnp.transpose(_conv_sample(state_t, xc_t, w), (1, 0, 2)).reshape(ms, -1)
            q8 = _pad_rows(q_s.reshape(ns, t_s, -1), q_rows)
            kt_new = jnp.transpose(_pad_rows(k_s.reshape(ns, t_s, -1), slots), (0, 2, 1))
            v_new = _pad_rows(v_s.reshape(ns, t_s * heads_d, -1), slots * heads_d)
            o8 = _diff_paged(pt, q8, kt_new, v_new, cache_kt, cache_v, w, layer=o)
            ob_s = o8[:, :t_s].reshape(ms, -1).astype(BF16)
            outs["conv_p"].append(st_p[:, st_p.shape[1] - n_conv_state:, :])
            outs["conv_s"].append(jnp.concatenate([state_conv[o][:, t_s:], xc_s.reshape(ns, t_s, -1)], axis=1))
            outs["dk_p"].append(kf_p.reshape(batch, seq, heads_d, 2, dh))
            outs["dv_p"].append(vf_p.reshape(batch, seq, heads_d, -1))
            outs["dk_s"].append(k_s.reshape(ns, t_s, heads_d, 2, dh))
            outs["dv_s"].append(v_s.reshape(ns, t_s, heads_d, -1))
        xp, ffn_st_p = _post(xp, oa_p, ob_p, w["w_out"], fw, sample=False, tm=tm_p, tiles_per_batch=tpb)
        st = state_ffn[l]
        zero = jnp.zeros((ns, t_s - 1, st.shape[-1]), F32)
        s1 = jnp.concatenate([st[:, 1:2], zero], axis=1).reshape(ms, -1)
        s2 = jnp.concatenate([st, zero[:, 1:]], axis=1).reshape(ms, -1)
        xs, g_s = _post(xs, oa_s, ob_s, w["w_out"], fw, sample=True, tm=tm_s, tiles_per_batch=1,
                        s1=s1, s2=s2, dec_seq=t_s)
        outs["ffn_p"].append(ffn_st_p[:, SUBLANES - 2:, :])
        outs["ffn_s"].append(g_s.reshape(ns, t_s, -1)[:, t_s - 2:, :])
    stk = lambda k: jnp.stack(outs[k])
    return (xp.reshape(batch, seq, d), xs.reshape(ns, t_s, d), stk("gv_s"), stk("lat_p"), stk("rope_p"),
            stk("lat_s"), stk("rope_s"), stk("conv_p"), stk("conv_s"), stk("dk_p"), stk("dv_p"),
            stk("dk_s"), stk("dv_s"), stk("ffn_p"), stk("ffn_s"))
```

```python
import functools
import math

import jax
import jax.numpy as jnp
from jax import lax
from jax.experimental import pallas as pl
from jax.experimental.pallas import tpu as pltpu

F32 = jnp.float32
BF16 = jnp.bfloat16

EPS = 1e-6
NEG_INF = -1e30
MLA_THETA = 10000.0
DIFF_THETA = 500000.0
GMLP_CHUNK = 128
LANES = 128
SUBLANES = 8
VMEM_LIMIT_BYTES = 56 * 1024 * 1024
LOG2_E = math.log2(math.e)
TOKEN_TILE = 512
ATTN_TILE = 256
FFN_CHUNK = 256
PAGES_PER_STEP = 16
MLA_PAGES_PER_STEP = 64


def _cparams(n_axes):
    return pltpu.CompilerParams(dimension_semantics=("arbitrary",) * n_axes,
                                vmem_limit_bytes=VMEM_LIMIT_BYTES)


def _const_spec(shape):
    zeros = (0,) * len(shape)
    return pl.BlockSpec(shape, lambda *_: zeros, pipeline_mode=pl.Buffered(1))


def _dot(a, b):
    return jnp.dot(a, b, preferred_element_type=F32)


def _dot_nt(a, b):
    return lax.dot_general(a, b, (((1,), (1,)), ((), ())), preferred_element_type=F32)


def _rms(x, g):
    ms = jnp.mean(x * x, axis=-1, keepdims=True)
    return x * lax.rsqrt(ms + EPS) * g


def _dot_f32_by_indicator(x, e):
    hi = x.astype(BF16)
    lo = (x - hi.astype(F32)).astype(BF16)
    return _dot(hi, e) + _dot(lo, e)


def _even_pre_kernel(x_ref, gmix_ref, win_ref, gv_ref, wmix_ref, bmix_ref, gcq_ref, wuq_ref,
                     gckv_ref, wukv_ref, gqn_ref, gknn_ref, gknr_ref, cs_ref, sn_ref, eq_ref,
                     ek_ref, *out_refs, sample, groups, heads, nope, rope_d, q_rank, kv_rank):
    if sample:
        oa_ref, vout_ref, lat_ref, rope_ref, q_ref = out_refs
    else:
        oa_ref, lat_ref, rope_ref, q_ref, k_ref, v_ref = out_refs
    tm = x_ref.shape[0]
    a_w = gv_ref.shape[1]
    gw = a_w // groups
    mb = wmix_ref.shape[1]
    hd = nope + rope_d
    n_nope = heads * nope
    n_rope = heads * rope_d

    x = x_ref[...]
    h = _rms(x, gmix_ref[...]).astype(BF16)
    z = _dot(h, win_ref[...])

    u = jax.nn.gelu(z[:, :a_w])
    v = jax.nn.gelu(z[:, a_w:2 * a_w])
    parts = []
    for g in range(groups):
        vg = v[:, g * gw:(g + 1) * gw]
        parts.append(vg * lax.rsqrt(jnp.mean(vg * vg, axis=-1, keepdims=True) + EPS))
    vn = jnp.concatenate(parts, axis=-1) * gv_ref[...]
    if sample:
        vout_ref[...] = vn
    for c in range(tm // mb):
        rows = slice(c * mb, (c + 1) * mb)
        for g in range(groups):
            cols = slice(g * gw, (g + 1) * gw)
            mix = _dot(wmix_ref[g], vn[rows, cols].astype(BF16)) + bmix_ref[g]
            oa_ref[rows, cols] = (u[rows, cols] * mix).astype(oa_ref.dtype)

    o = 2 * a_w
    r = _rms(z[:, o:o + q_rank], gcq_ref[...]).astype(BF16)
    qall = _dot(r, wuq_ref[...])
    cs = cs_ref[...]
    sn = sn_ref[...]
    q_nope = qall[:, :n_nope]
    q_rope = qall[:, n_nope:n_nope + n_rope] * cs + qall[:, n_nope + n_rope:] * sn
    qq = jnp.concatenate([q_nope, q_rope], axis=-1)
    ss = _dot_f32_by_indicator(qq * qq, eq_ref[...])
    qq = qq * lax.rsqrt(ss * (1.0 / hd) + EPS) * gqn_ref[...]

    o += q_rank
    c = _rms(z[:, o:o + kv_rank], gckv_ref[...])
    lat_ref[...] = c
    o += kv_rank
    kpe2 = z[:, o:o + 2 * rope_d] * cs[:, :2 * rope_d] + z[:, o + 2 * rope_d:o + 4 * rope_d] * sn[:, :2 * rope_d]
    rope_ref[...] = kpe2[:, :rope_d]

    if sample:
        q_ref[...] = qq
        return

    pad = jnp.zeros((tm, q_ref.shape[2] - hd), q_ref.dtype)
    for hh in range(heads):
        q_ref[hh, :, 0:nope] = qq[:, hh * nope:(hh + 1) * nope].astype(q_ref.dtype)
        q_ref[hh, :, nope:hd] = qq[:, n_nope + hh * rope_d:n_nope + (hh + 1) * rope_d].astype(q_ref.dtype)
        if pad.shape[1]:
            q_ref[hh, :, hd:] = pad

    kv = _dot(c.astype(BF16), wukv_ref[...])
    kn = kv[:, :n_nope]
    ssr = jnp.sum(kpe2[:, :rope_d] * kpe2[:, :rope_d], axis=-1, keepdims=True)
    ssk = _dot_f32_by_indicator(kn * kn, ek_ref[...]) + ssr
    inv = lax.rsqrt(ssk * (1.0 / hd) + EPS)
    kn = kn * inv[:, :n_nope] * gknn_ref[...]
    kr = jnp.concatenate([kpe2] * (heads // 2), axis=-1) * inv[:, n_nope:] * gknr_ref[...]
    for hh in range(heads):
        k_ref[hh, :, 0:nope] = kn[:, hh * nope:(hh + 1) * nope].astype(k_ref.dtype)
        k_ref[hh, :, nope:hd] = kr[:, hh * rope_d:(hh + 1) * rope_d].astype(k_ref.dtype)
        if pad.shape[1]:
            k_ref[hh, :, hd:] = pad
    v_ref[...] = kv[:, n_nope:].astype(v_ref.dtype)


def _even_pre(x, tabs, w, *, sample, tm, pos_rows):
    m, d = x.shape
    dims = w["dims"]
    heads, nope, rope_d = dims["heads"], dims["nope"], dims["rope_d"]
    a_w, vdim = dims["a_w"], dims["vdim"]
    hd = nope + rope_d
    n_cat = heads * hd
    hdp = -(-hd // LANES) * LANES
    nt = m // tm
    pos_blocks = pos_rows // tm
    row = lambda i: (i, 0)
    pos = lambda i: (i % pos_blocks, 0)
    consts = [w["g_mix"], w["w_in"], w["g_v"], w["wmix_s" if sample else "wmix_p"],
              w["bmix_s" if sample else "bmix_p"], w["g_cq"], w["w_uq"], w["g_ckv"], w["w_ukv"],
              w["g_qn"], w["g_kn_n"], w["g_kn_r"]]
    in_specs = ([pl.BlockSpec((tm, d), row)] + [_const_spec(c.shape) for c in consts]
                + [pl.BlockSpec((tm, heads * rope_d), pos), pl.BlockSpec((tm, heads * rope_d), pos),
                   _const_spec(w["e_q"].shape), _const_spec(w["e_k"].shape)])
    if sample:
        out_shape = [jax.ShapeDtypeStruct((m, a_w), BF16), jax.ShapeDtypeStruct((m, a_w), F32),
                     jax.ShapeDtypeStruct((m, dims["kv_rank"]), F32), jax.ShapeDtypeStruct((m, rope_d), F32),
                     jax.ShapeDtypeStruct((m, n_cat), F32)]
        out_specs = [pl.BlockSpec((tm, a_w), row), pl.BlockSpec((tm, a_w), row),
                     pl.BlockSpec((tm, dims["kv_rank"]), row), pl.BlockSpec((tm, rope_d), row),
                     pl.BlockSpec((tm, n_cat), row)]
    else:
        out_shape = [jax.ShapeDtypeStruct((m, a_w), BF16),
                     jax.ShapeDtypeStruct((m, dims["kv_rank"]), F32), jax.ShapeDtypeStruct((m, rope_d), F32),
                     jax.ShapeDtypeStruct((heads, m, hdp), BF16), jax.ShapeDtypeStruct((heads, m, hdp), BF16),
                     jax.ShapeDtypeStruct((m, heads * vdim), BF16)]
        out_specs = [pl.BlockSpec((tm, a_w), row),
                     pl.BlockSpec((tm, dims["kv_rank"]), row), pl.BlockSpec((tm, rope_d), row),
                     pl.BlockSpec((heads, tm, hdp), lambda i: (0, i, 0)),
                     pl.BlockSpec((heads, tm, hdp), lambda i: (0, i, 0)),
                     pl.BlockSpec((tm, heads * vdim), row)]
    kern = functools.partial(_even_pre_kernel, sample=sample, groups=dims["groups"], heads=heads,
                             nope=nope, rope_d=rope_d, q_rank=dims["q_rank"], kv_rank=dims["kv_rank"])
    return pl.pallas_call(
        kern, grid=(nt,), in_specs=in_specs, out_specs=out_specs, out_shape=out_shape,
        compiler_params=_cparams(1), name="even_pre_s" if sample else "even_pre_p",
    )(x, *consts, tabs["mla_cs"], tabs["mla_sn"], w["e_q"], w["e_k"])


def _group_norm_rope(q, g, cs, sn, dh):
    tm, n = q.shape
    lane = lax.broadcasted_iota(jnp.int32, (tm, LANES), 1)
    per_vreg = LANES // dh
    cols = []
    for c in range(n // LANES):
        qc = q[:, c * LANES:(c + 1) * LANES]
        sq = qc * qc
        inv = jnp.zeros_like(qc)
        for s in range(per_vreg):
            sel = (lane >= s * dh) & (lane < (s + 1) * dh)
            ms = jnp.sum(jnp.where(sel, sq, 0.0), axis=-1, keepdims=True) * (1.0 / dh)
            inv = jnp.where(sel, lax.rsqrt(ms + EPS), inv)
        cols.append(qc * inv)
    qn = jnp.concatenate(cols, axis=-1) * g
    half = dh // 8
    lane_d = lax.broadcasted_iota(jnp.int32, (tm, n), 1) % dh
    partner = jnp.where(lane_d < half, pltpu.roll(qn, n - half, 1), pltpu.roll(qn, half, 1))
    return qn * cs + partner * sn


def _odd_pre_kernel(x_ref, gmix_ref, win_ref, gq_ref, gk_ref, cs_ref, sn_ref, *rest,
                    sample, c_w, qk_w, dh, tiles_per_batch):
    if sample:
        xc_ref, q_ref, k_ref, v_ref = rest
    else:
        (cw_ref, cb_ref, lng_ref, lnb_ref,
         oc_ref, q_ref, kf_ref, kb_ref, vf_ref, vb_ref, st_ref, xbuf) = rest
    tm = x_ref.shape[0]
    x = x_ref[...]
    h = _rms(x, gmix_ref[...]).astype(BF16)
    z = _dot(h, win_ref[...])
    xc = z[:, :c_w] * jax.nn.sigmoid(z[:, c_w:2 * c_w])
    o = 2 * c_w
    cs = cs_ref[...]
    sn = sn_ref[...]
    q = _group_norm_rope(z[:, o:o + qk_w], gq_ref[...], cs, sn, dh)
    k = _group_norm_rope(z[:, o + qk_w:o + 2 * qk_w], gk_ref[...], cs, sn, dh)
    v = z[:, o + 2 * qk_w:]
    if sample:
        xc_ref[...] = xc
        q_ref[...] = q
        k_ref[...] = k
        v_ref[...] = v
        return
    q_ref[...] = q.astype(q_ref.dtype)
    kf_ref[...] = k
    kb_ref[...] = k.astype(kb_ref.dtype)
    vf_ref[...] = v
    vb_ref[...] = v.astype(vb_ref.dtype)

    n_taps = cw_ref.shape[0]
    halo = xbuf.shape[0] - tm
    first = (pl.program_id(0) % tiles_per_batch) == 0

    @pl.when(first)
    def _():
        xbuf[0:halo, :] = jnp.zeros((halo, c_w), F32)

    @pl.when(jnp.logical_not(first))
    def _():
        xbuf[0:halo, :] = xbuf[tm:tm + halo, :]

    xbuf[halo:halo + tm, :] = xc
    st_ref[0] = xc[tm - halo:, :]
    y = jnp.zeros((tm, c_w), F32) + cb_ref[...]
    base = halo - (n_taps - 1)
    for t in range(n_taps):
        y = y + cw_ref[t:t + 1, :] * xbuf[base + t:base + t + tm, :]
    mu = jnp.mean(y, axis=-1, keepdims=True)
    dlt = y - mu
    yn = dlt * lax.rsqrt(jnp.mean(dlt * dlt, axis=-1, keepdims=True) + EPS) * lng_ref[...] + lnb_ref[...]
    oc_ref[...] = (yn * jax.nn.sigmoid(yn)).astype(oc_ref.dtype)


def _odd_pre(x, tabs, w, *, sample, tm, pos_rows, tiles_per_batch):
    m, d = x.shape
    dims = w["dims"]
    c_w, qk_w, v_w, dh = dims["c_w"], dims["qk_w"], dims["v_w"], dims["dh"]
    nt = m // tm
    pos_blocks = pos_rows // tm
    row = lambda i: (i, 0)
    pos = lambda i: (i % pos_blocks, 0)
    consts = [w["g_mix"], w["w_in"], w["g_q"], w["g_k"]]
    in_specs = ([pl.BlockSpec((tm, d), row)] + [_const_spec(c.shape) for c in consts]
                + [pl.BlockSpec((tm, qk_w), pos), pl.BlockSpec((tm, qk_w), pos)])
    args = [x] + consts + [tabs["diff_cs"], tabs["diff_sn"]]
    scratch = []
    if sample:
        out_shape = [jax.ShapeDtypeStruct((m, c_w), F32), jax.ShapeDtypeStruct((m, qk_w), F32),
                     jax.ShapeDtypeStruct((m, qk_w), F32), jax.ShapeDtypeStruct((m, v_w), F32)]
        out_specs = [pl.BlockSpec((tm, c_w), row), pl.BlockSpec((tm, qk_w), row),
                     pl.BlockSpec((tm, qk_w), row), pl.BlockSpec((tm, v_w), row)]
    else:
        halo = 32
        conv = [w["conv_w"], w["conv_b"], w["ln_g"], w["ln_b"]]
        in_specs += [_const_spec(c.shape) for c in conv]
        args += conv
        nb = nt // tiles_per_batch
        out_shape = [jax.ShapeDtypeStruct((m, c_w), BF16), jax.ShapeDtypeStruct((m, qk_w), BF16),
                     jax.ShapeDtypeStruct((m, qk_w), F32), jax.ShapeDtypeStruct((m, qk_w), BF16),
                     jax.ShapeDtypeStruct((m, v_w), F32), jax.ShapeDtypeStruct((m, v_w), BF16),
                     jax.ShapeDtypeStruct((nb, halo, c_w), F32)]
        out_specs = [pl.BlockSpec((tm, c_w), row), pl.BlockSpec((tm, qk_w), row),
                     pl.BlockSpec((tm, qk_w), row), pl.BlockSpec((tm, qk_w), row),
                     pl.BlockSpec((tm, v_w), row), pl.BlockSpec((tm, v_w), row),
                     pl.BlockSpec((1, halo, c_w), lambda i: (i // tiles_per_batch, 0, 0))]
        scratch = [pltpu.VMEM((tm + halo, c_w), F32)]
    kern = functools.partial(_odd_pre_kernel, sample=sample, c_w=c_w, qk_w=qk_w, dh=dh,
                             tiles_per_batch=tiles_per_batch)
    return pl.pallas_call(
        kern, grid=(nt,), in_specs=in_specs, out_specs=out_specs, out_shape=out_shape,
        scratch_shapes=scratch, compiler_params=_cparams(1),
        name="odd_pre_s" if sample else "odd_pre_p",
    )(*args)


def _conv_sample_kernel(st_ref, xc_ref, cw_ref, cb_ref, lng_ref, lnb_ref, oc_ref):
    n_state = st_ref.shape[0]
    n_taps = cw_ref.shape[0]
    for t in range(xc_ref.shape[0]):
        y = jnp.zeros(oc_ref.shape[1:], F32) + cb_ref[...]
        for kk in range(n_taps):
            idx = t + kk
            src = st_ref[idx] if idx < n_state else xc_ref[idx - n_state]
            y = y + cw_ref[kk:kk + 1, :] * src
        mu = jnp.mean(y, axis=-1, keepdims=True)
        dlt = y - mu
        yn = dlt * lax.rsqrt(jnp.mean(dlt * dlt, axis=-1, keepdims=True) + EPS) * lng_ref[...] + lnb_ref[...]
        oc_ref[t] = (yn * jax.nn.sigmoid(yn)).astype(oc_ref.dtype)


def _conv_sample(state_t, xc_t, w):
    t, n, c = xc_t.shape
    args = [state_t, xc_t, w["conv_w"], w["conv_b"], w["ln_g"], w["ln_b"]]
    return pl.pallas_call(
        _conv_sample_kernel, grid=(1,), in_specs=[_const_spec(a.shape) for a in args],
        out_specs=pl.BlockSpec((t, n, c), lambda i: (0, 0, 0)), out_shape=jax.ShapeDtypeStruct((t, n, c), BF16),
        compiler_params=_cparams(1), name="conv_sample",
    )(*args)


def _softmax_step(s, m, l):
    m_new = jnp.maximum(m, jnp.max(s, axis=-1, keepdims=True))
    alpha = jnp.exp(m - m_new)
    p = jnp.exp(s - m_new)
    return m_new, alpha, p, alpha * l + jnp.sum(p, axis=-1, keepdims=True)


def _flash_t_update(st, idx, vt, m_ref, l_ref, acc_ref, c):
    m_prev = m_ref[idx]
    m_new = jnp.maximum(m_prev, jnp.max(st, axis=0, keepdims=True))
    alpha = jnp.exp2((m_prev - m_new) * c)
    p = jnp.exp2((st - m_new) * c)
    m_ref[idx] = m_new
    l_ref[idx] = alpha * l_ref[idx] + jnp.sum(p, axis=0, keepdims=True)
    acc_ref[idx] = alpha * acc_ref[idx] + _dot(vt, p.astype(BF16))


def _flash_init(m_ref, l_ref, acc_ref):
    m_ref[...] = jnp.full(m_ref.shape, NEG_INF, F32)
    l_ref[...] = jnp.zeros(l_ref.shape, F32)
    acc_ref[...] = jnp.zeros(acc_ref.shape, F32)


def _build_vt(v_ref, vt_ref, heads, vdim):
    tk = vt_ref.shape[-1]
    for hh in range(heads):
        for jb in range(vt_ref.shape[1]):
            blk = v_ref[jb * tk:(jb + 1) * tk, hh * vdim:(hh + 1) * vdim].astype(F32)
            vt_ref[hh, jb] = blk.T.astype(vt_ref.dtype)


def _diag_mask(tk, strip, st):
    key = lax.broadcasted_iota(jnp.int32, (tk, strip), 0)
    qry = lax.broadcasted_iota(jnp.int32, (tk, strip), 1) + st * strip
    return key <= qry


def _mla_flash_kernel(q_ref, k_ref, v_ref, o_ref, qt_ref, vt_ref, m_ref, l_ref, acc_ref, *, heads, vdim, c, strip):
    tq = q_ref.shape[1]
    n_strip = tq // strip
    i = pl.program_id(1)

    @pl.when(i == 0)
    def _():
        _build_vt(v_ref, vt_ref, heads, vdim)

    for hh in range(heads):
        qt_ref[hh] = q_ref[hh].astype(F32).T.astype(qt_ref.dtype)
    _flash_init(m_ref, l_ref, acc_ref)

    def block(j, diag):
        start = pl.multiple_of(j * tq, tq)
        for hh in range(heads):
            kb = k_ref[hh, pl.ds(start, tq), :]
            vt = vt_ref[hh, j]
            for st in range(n_strip):
                s_t = _dot(kb, qt_ref[hh, :, st * strip:(st + 1) * strip])
                if diag:
                    s_t = jnp.where(_diag_mask(tq, strip, st), s_t, NEG_INF)
                _flash_t_update(s_t, hh * n_strip + st, vt, m_ref, l_ref, acc_ref, c)

    def body(j, carry):
        block(j, False)
        return carry

    lax.fori_loop(0, i, body, 0)
    block(i, True)
    for hh in range(heads):
        for st in range(n_strip):
            idx = hh * n_strip + st
            o_t = acc_ref[idx] / l_ref[idx]
            o_ref[st * strip:(st + 1) * strip, hh * vdim:(hh + 1) * vdim] = o_t.T.astype(o_ref.dtype)


def _flash_scratch(n_chain, heads, seq, tq, vdim, strip):
    return [pltpu.VMEM((heads, seq // tq, vdim, tq), BF16),
            pltpu.VMEM((n_chain, 1, strip), F32), pltpu.VMEM((n_chain, 1, strip), F32),
            pltpu.VMEM((n_chain, vdim, strip), F32)]


def _mla_flash(q, k, v, *, batch, seq, vdim, scale):
    heads, m, hd = q.shape
    tq = min(ATTN_TILE, seq)
    nq = seq // tq
    strip = tq
    kern = functools.partial(_mla_flash_kernel, heads=heads, vdim=vdim, c=scale * LOG2_E, strip=strip)
    return pl.pallas_call(
        kern, grid=(batch, nq),
        in_specs=[pl.BlockSpec((heads, tq, hd), lambda b, i: (0, b * nq + i, 0)),
                  pl.BlockSpec((heads, seq, hd), lambda b, i: (0, b, 0)),
                  pl.BlockSpec((seq, heads * vdim), lambda b, i: (b, 0))],
        out_specs=pl.BlockSpec((tq, heads * vdim), lambda b, i: (b * nq + i, 0)),
        out_shape=jax.ShapeDtypeStruct((m, heads * vdim), BF16),
        scratch_shapes=[pltpu.VMEM((heads, hd, tq), BF16)]
                       + _flash_scratch(heads * (tq // strip), heads, seq, tq, vdim, strip),
        compiler_params=_cparams(2), name="mla_flash",
    )(q, k, v)


def _diff_lambda(lq1_ref, lk1_ref, lq2_ref, lk2_ref, lam_init):
    a = jnp.sum(lq1_ref[...] * lk1_ref[...], axis=-1, keepdims=True)
    b = jnp.sum(lq2_ref[...] * lk2_ref[...], axis=-1, keepdims=True)
    return jnp.exp(a) - jnp.exp(b) + lam_init


def _diff_finish(o1, o2, lam, gsub, lam_init):
    o = o1 - lam * o2
    return o * lax.rsqrt(jnp.mean(o * o, axis=-1, keepdims=True) + EPS) * gsub * (1.0 - lam_init)


def _diff_flash_kernel(q_ref, k_ref, v_ref, lq1_ref, lk1_ref, lq2_ref, lk2_ref, gsub_ref, o_ref,
                       qm_ref, vt_ref, m_ref, l_ref, acc_ref, *, heads, dh, vdim, c, lam_init, strip):
    tq = q_ref.shape[0]
    n_strip = tq // strip
    i = pl.program_id(1)

    @pl.when(i == 0)
    def _():
        _build_vt(v_ref, vt_ref, heads, vdim)

    lane = lax.broadcasted_iota(jnp.int32, (tq, 2 * dh), 1)
    _flash_init(m_ref, l_ref, acc_ref)
    for hh in range(heads):
        qh = q_ref[:, hh * 2 * dh:(hh + 1) * 2 * dh]
        zero = jnp.zeros_like(qh)
        qm_ref[2 * hh] = jnp.where(lane < dh, qh, zero)
        qm_ref[2 * hh + 1] = jnp.where(lane < dh, zero, qh)

    def block(j, diag):
        start = pl.multiple_of(j * tq, tq)
        for hh in range(heads):
            kb = k_ref[pl.ds(start, tq), hh * 2 * dh:(hh + 1) * 2 * dh]
            vt = vt_ref[hh, j]
            for comp in range(2):
                for st in range(n_strip):
                    s_t = _dot_nt(kb, qm_ref[2 * hh + comp, st * strip:(st + 1) * strip, :])
                    if diag:
                        s_t = jnp.where(_diag_mask(tq, strip, st), s_t, NEG_INF)
                    _flash_t_update(s_t, (2 * hh + comp) * n_strip + st, vt, m_ref, l_ref, acc_ref, c)

    def body(j, carry):
        block(j, False)
        return carry

    lax.fori_loop(0, i, body, 0)
    block(i, True)
    lam = _diff_lambda(lq1_ref, lk1_ref, lq2_ref, lk2_ref, lam_init)
    for hh in range(heads):
        for st in range(n_strip):
            i1 = (2 * hh) * n_strip + st
            i2 = (2 * hh + 1) * n_strip + st
            o1 = (acc_ref[i1] / l_ref[i1]).T
            o2 = (acc_ref[i2] / l_ref[i2]).T
            out = _diff_finish(o1, o2, lam, gsub_ref[...], lam_init)
            o_ref[st * strip:(st + 1) * strip, hh * vdim:(hh + 1) * vdim] = out.astype(o_ref.dtype)


def _diff_flash(q, k, v, w, *, batch, seq):
    m, qk_w = q.shape
    dims = w["dims"]
    heads, dh, vdim = dims["heads"], dims["dh"], dims["vdim"]
    tq = min(ATTN_TILE, seq)
    nq = seq // tq
    small = [w["lq1"], w["lk1"], w["lq2"], w["lk2"], w["g_sub"]]
    strip = LANES
    kern = functools.partial(_diff_flash_kernel, heads=heads, dh=dh, vdim=vdim, c=dh ** -0.5 * LOG2_E,
                             lam_init=w["lam_init"], strip=strip)
    return pl.pallas_call(
        kern, grid=(batch, nq),
        in_specs=[pl.BlockSpec((tq, qk_w), lambda b, i: (b * nq + i, 0)),
                  pl.BlockSpec((seq, qk_w), lambda b, i: (b, 0)),
                  pl.BlockSpec((seq, heads * vdim), lambda b, i: (b, 0))]
                 + [_const_spec(a.shape) for a in small],
        out_specs=pl.BlockSpec((tq, heads * vdim), lambda b, i: (b * nq + i, 0)),
        out_shape=jax.ShapeDtypeStruct((m, heads * vdim), BF16),
        scratch_shapes=[pltpu.VMEM((2 * heads, tq, 2 * dh), BF16)]
                       + _flash_scratch(2 * heads * (tq // strip), heads, seq, tq, vdim, strip),
        compiler_params=_cparams(2), name="diff_flash",
    )(q, k, v, *small)


def _page_specs(block, layer, n_pages, ppstep):
    def make(i):
        return pl.BlockSpec(block, lambda b, g, pt: (layer, pt[b * n_pages + g * ppstep + i], 0, 0))
    return [make(i) for i in range(ppstep)]


def _mla_paged_kernel(pt_ref, q_ref, cnew_ref, rnew_ref, wukt_ref, wuv_ref, gn_ref, gr_ref, *rest,
                      ppstep, heads, nope, rope_d, scale):
    lat_refs = rest[:ppstep]
    rope_refs = rest[ppstep:2 * ppstep]
    o_ref, lhs_ref, qr_ref, m_ref, l_ref, acc_ref = rest[2 * ppstep:]
    g = pl.program_id(1)
    hd = nope + rope_d
    n_nope = heads * nope
    rows = q_ref.shape[1]

    @pl.when(g == 0)
    def _():
        q = q_ref[0]
        lhs_ref[0:n_nope, :] = wukt_ref[...]
        qa, qr = [], []
        for hh in range(heads):
            qg = (q[:, hh * nope:(hh + 1) * nope] * gn_ref[...]).astype(BF16)
            qa.append(_dot(qg, wukt_ref[hh * nope:(hh + 1) * nope, :]))
            qr.append(q[:, n_nope + hh * rope_d:n_nope + (hh + 1) * rope_d] * gr_ref[...])
        lhs_ref[n_nope:, :] = jnp.concatenate(qa, axis=0).astype(BF16)
        qr_ref[...] = jnp.concatenate(qr, axis=0).astype(BF16)
        _flash_init(m_ref, l_ref, acc_ref)

    def attend(cs, rts, mask):
        cb = [c.astype(BF16) for c in cs]
        n_cache = len(cb) - 1
        groups = [(jnp.concatenate(cb[i:i + 2], axis=0), jnp.concatenate(rts[i:i + 2], axis=1))
                  for i in range(0, n_cache - 1, 2)]
        groups += [(cb[i], rts[i]) for i in range(n_cache - n_cache % 2, len(cb))]
        scores = []
        for c16, rt in groups:
            slots = c16.shape[0]
            big = _dot_nt(lhs_ref[...], c16)
            ssr = jnp.sum(rt * rt, axis=0, keepdims=True)
            inv = []
            for hh in range(heads):
                kt = big[hh * nope:(hh + 1) * nope, :]
                ss = jnp.sum(kt * kt, axis=0, keepdims=True) + ssr
                inv.append(jnp.broadcast_to(lax.rsqrt(ss * (1.0 / hd) + EPS) * scale, (rows, slots)))
            s = big[n_nope:, :] + _dot(qr_ref[...], rt.astype(BF16))
            scores.append(s * jnp.concatenate(inv, axis=0))
        scores[-1] = jnp.where(mask, scores[-1], NEG_INF)
        s = jnp.concatenate(scores, axis=-1)
        m, alpha, p, l = _softmax_step(s, m_ref[...], l_ref[...])
        m_ref[...] = m
        l_ref[...] = l
        acc_ref[...] = alpha * acc_ref[...] + _dot(p.astype(BF16), jnp.concatenate(cb, axis=0))

    is_last = g == pl.num_programs(1) - 1
    slots = cnew_ref.shape[1]
    r_i = lax.broadcasted_iota(jnp.int32, (heads * rows, slots), 0) % rows
    new_mask = (lax.broadcasted_iota(jnp.int32, (heads * rows, slots), 1) <= r_i) & is_last
    attend([r[...] for r in lat_refs] + [cnew_ref[0]], [r[...] for r in rope_refs] + [rnew_ref[0]], new_mask)

    @pl.when(is_last)
    def _():
        vdim = wuv_ref.shape[2]
        ol = (acc_ref[...] / l_ref[...]).astype(BF16)
        for hh in range(heads):
            o_ref[0, :, hh * vdim:(hh + 1) * vdim] = _dot(ol[hh * rows:(hh + 1) * rows, :], wuv_ref[hh])


def _mla_paged(pt, q8, c_new, rt_new, cache_lat, cache_rope_t, w, *, layer):
    ns, rows, n_cat = q8.shape
    dims = w["dims"]
    heads, nope, rope_d, vdim, rank = dims["heads"], dims["nope"], dims["rope_d"], dims["vdim"], dims["kv_rank"]
    slots = cache_lat.shape[2]
    n_pages = pt.shape[0] // ns
    ppstep = math.gcd(MLA_PAGES_PER_STEP, n_pages)
    ng = n_pages // ppstep
    per_seq = lambda b, g, pt: (b, 0, 0)
    cm = lambda nd: (lambda b, g, pt: (0,) * nd)
    consts = [w["w_uk_t"], w["w_uv"], w["g_kn_n"][:, :nope], w["g_kn_r"][:, :rope_d]]
    in_specs = ([pl.BlockSpec((1, rows, n_cat), per_seq),
                 pl.BlockSpec((1, slots, rank), per_seq), pl.BlockSpec((1, rope_d, slots), per_seq)]
                + [pl.BlockSpec(c.shape, cm(c.ndim), pipeline_mode=pl.Buffered(1)) for c in consts]
                + _page_specs((None, None, slots, rank), layer, n_pages, ppstep)
                + _page_specs((None, None, rope_d, slots), layer, n_pages, ppstep))
    kern = functools.partial(_mla_paged_kernel, ppstep=ppstep, heads=heads, nope=nope, rope_d=rope_d,
                             scale=(nope + rope_d) ** -0.5)
    grid_spec = pltpu.PrefetchScalarGridSpec(
        num_scalar_prefetch=1, grid=(ns, ng), in_specs=in_specs,
        out_specs=pl.BlockSpec((1, rows, heads * vdim), per_seq),
        scratch_shapes=[pltpu.VMEM((heads * (nope + rows), rank), BF16),
                        pltpu.VMEM((heads * rows, rope_d), BF16),
                        pltpu.VMEM((heads * rows, 1), F32), pltpu.VMEM((heads * rows, 1), F32),
                        pltpu.VMEM((heads * rows, rank), F32)])
    return pl.pallas_call(
        kern, grid_spec=grid_spec, out_shape=jax.ShapeDtypeStruct((ns, rows, heads * vdim), F32),
        compiler_params=_cparams(2), name="mla_paged",
    )(pt, q8, c_new, rt_new, *consts, *([cache_lat] * ppstep), *([cache_rope_t] * ppstep))


def _diff_paged_kernel(pt_ref, q_ref, knew_ref, vnew_ref, lq1_ref, lk1_ref, lq2_ref, lk2_ref, gsub_ref,
                       *rest, ppstep, heads, dh, vdim, scale, lam_init):
    k_refs = rest[:ppstep]
    v_refs = rest[ppstep:2 * ppstep]
    o_ref, qbd_ref, m_ref, l_ref, acc_ref = rest[2 * ppstep:]
    g = pl.program_id(1)
    rows = q_ref.shape[1]
    n_maps = 2 * heads
    qk_w = n_maps * dh

    @pl.when(g == 0)
    def _():
        q = q_ref[0]
        lane = lax.broadcasted_iota(jnp.int32, (rows, qk_w), 1)
        zero = jnp.zeros_like(q)
        blocks = [jnp.where((lane >= i * dh) & (lane < (i + 1) * dh), q, zero) for i in range(n_maps)]
        qbd_ref[...] = jnp.concatenate(blocks, axis=0).astype(BF16)
        _flash_init(m_ref, l_ref, acc_ref)

    def attend(kt_refs, vv_refs, mask):
        slots = kt_refs[0][0].shape[-1]
        full = (slice(None), slice(None))
        parts = [_dot(qbd_ref[...], r[sl + full].astype(BF16)) * scale for r, sl in kt_refs]
        parts[-1] = jnp.where(mask, parts[-1], NEG_INF)
        s = jnp.concatenate(parts, axis=-1)
        m, alpha, p, l = _softmax_step(s, m_ref[...], l_ref[...])
        m_ref[...] = m
        l_ref[...] = l
        p = p.astype(BF16)
        for hh in range(heads):
            hr = slice(hh * 2 * rows, (hh + 1) * 2 * rows)
            vs = [r[sl + (pl.ds(hh, slots, stride=heads), slice(None))].astype(BF16) for r, sl in vv_refs]
            acc_ref[hr, :] = alpha[hr, :] * acc_ref[hr, :] + _dot(p[hr, :], jnp.concatenate(vs, axis=0))

    is_last = g == pl.num_programs(1) - 1
    slots = knew_ref.shape[-1]
    r_i = lax.broadcasted_iota(jnp.int32, (n_maps * rows, slots), 0) % rows
    new_mask = (lax.broadcasted_iota(jnp.int32, (n_maps * rows, slots), 1) <= r_i) & is_last
    attend([(r, ()) for r in k_refs] + [(knew_ref, (0,))], [(r, ()) for r in v_refs] + [(vnew_ref, (0,))],
           new_mask)

    @pl.when(is_last)
    def _():
        lam = _diff_lambda(lq1_ref, lk1_ref, lq2_ref, lk2_ref, lam_init)
        o = acc_ref[...] / l_ref[...]
        for hh in range(heads):
            o1 = o[2 * hh * rows:(2 * hh + 1) * rows, :]
            o2 = o[(2 * hh + 1) * rows:(2 * hh + 2) * rows, :]
            o_ref[0, :, hh * vdim:(hh + 1) * vdim] = _diff_finish(o1, o2, lam, gsub_ref[...], lam_init)


def _diff_paged(pt, q8, kt_new, v_new, cache_kt, cache_v, w, *, layer):
    ns, rows, qk_w = q8.shape
    dims = w["dims"]
    heads, dh, vdim = dims["heads"], dims["dh"], dims["vdim"]
    slots = cache_kt.shape[3]
    n_pages = pt.shape[0] // ns
    ppstep = min(PAGES_PER_STEP, n_pages)
    ng = n_pages // ppstep
    per_seq = lambda b, g, pt: (b, 0, 0)
    cm = lambda nd: (lambda b, g, pt: (0,) * nd)
    small = [w["lq1"], w["lk1"], w["lq2"], w["lk2"], w["g_sub"]]
    in_specs = ([pl.BlockSpec((1, rows, qk_w), per_seq),
                 pl.BlockSpec((1, qk_w, slots), per_seq), pl.BlockSpec((1, slots * heads, vdim), per_seq)]
                + [pl.BlockSpec(c.shape, cm(c.ndim), pipeline_mode=pl.Buffered(1)) for c in small]
                + _page_specs((None, None, qk_w, slots), layer, n_pages, ppstep)
                + _page_specs((None, None, slots * heads, vdim), layer, n_pages, ppstep))
    kern = functools.partial(_diff_paged_kernel, ppstep=ppstep, heads=heads, dh=dh, vdim=vdim,
                             scale=dh ** -0.5, lam_init=w["lam_init"])
    grid_spec = pltpu.PrefetchScalarGridSpec(
        num_scalar_prefetch=1, grid=(ns, ng), in_specs=in_specs,
        out_specs=pl.BlockSpec((1, rows, heads * vdim), per_seq),
        scratch_shapes=[pltpu.VMEM((2 * heads * rows, qk_w), BF16),
                        pltpu.VMEM((2 * heads * rows, 1), F32), pltpu.VMEM((2 * heads * rows, 1), F32),
                        pltpu.VMEM((2 * heads * rows, vdim), F32)])
    return pl.pallas_call(
        kern, grid_spec=grid_spec, out_shape=jax.ShapeDtypeStruct((ns, rows, heads * vdim), F32),
        compiler_params=_cparams(2), name="diff_paged",
    )(pt, q8, kt_new, v_new, *small, *([cache_kt] * ppstep), *([cache_v] * ppstep))


def _post_kernel(x_ref, oa_ref, ob_ref, wo_ref, gffn_ref, wup_ref, cw_ref, cb_ref, wdn_ref, *rest,
                 sample, d_ff, fc, tiles_per_batch, dec_seq):
    if sample:
        s1_ref, s2_ref, xo_ref, g_ref, gbuf, act_ref = rest
    else:
        xo_ref, st_ref, gbuf, act_ref, carry = rest
    tm = x_ref.shape[0]
    half = oa_ref.shape[1]
    pad = SUBLANES
    x1 = x_ref[...] + _dot(oa_ref[...], wo_ref[0:half, :]) + _dot(ob_ref[...], wo_ref[half:, :])
    h = _rms(x1, gffn_ref[...]).astype(BF16)
    if sample:
        t_idx = lax.broadcasted_iota(jnp.int32, (tm, fc), 0) % dec_seq
        gbuf[0:pad, :] = jnp.zeros((pad, fc), F32)
    else:
        @pl.when((pl.program_id(0) % tiles_per_batch) == 0)
        def _():
            carry[...] = jnp.zeros(carry.shape, F32)
    for c in range(d_ff // fc):
        cols = slice(c * fc, (c + 1) * fc)
        gate = _dot(h, wup_ref[:, c * fc:(c + 1) * fc])
        up = _dot(h, wup_ref[:, d_ff + c * fc:d_ff + (c + 1) * fc])
        if sample:
            g_ref[:, cols] = gate
            gbuf[pad:pad + tm, :] = gate
            p1 = jnp.where(t_idx >= 1, gbuf[pad - 1:pad - 1 + tm, :], s1_ref[:, cols])
            p2 = jnp.where(t_idx >= 2, gbuf[pad - 2:pad - 2 + tm, :], s2_ref[:, cols])
        else:
            gbuf[0:pad, :] = carry[c]
            gbuf[pad:pad + tm, :] = gate
            carry[c] = gate[tm - pad:, :]
            st_ref[0, :, cols] = gate[tm - pad:, :]
            p1 = gbuf[pad - 1:pad - 1 + tm, :]
            p2 = gbuf[pad - 2:pad - 2 + tm, :]
        y = cb_ref[:, cols] + cw_ref[0:1, cols] * p2 + cw_ref[1:2, cols] * p1 + cw_ref[2:3, cols] * gate
        act_ref[:, cols] = (y * jax.nn.sigmoid(y) * up).astype(act_ref.dtype)
    xo_ref[...] = x1 + _dot(act_ref[...], wdn_ref[...])


def _post(x, oa, ob, w_out, fw, *, sample, tm, tiles_per_batch, s1=None, s2=None, dec_seq=1):
    m, d = x.shape
    half = oa.shape[1]
    d_ff = fw["conv_b"].shape[1]
    fc = FFN_CHUNK
    assert d_ff % fc == 0 and fw["conv_w"].shape[0] == 3
    nt = m // tm
    row = lambda i: (i, 0)
    consts = [w_out, fw["g_ffn"], fw["w_up"], fw["conv_w"], fw["conv_b"], fw["w_down"]]
    in_specs = ([pl.BlockSpec((tm, d), row), pl.BlockSpec((tm, half), row), pl.BlockSpec((tm, half), row)]
                + [_const_spec(c.shape) for c in consts])
    args = [x, oa, ob] + consts
    scratch = [pltpu.VMEM((tm + SUBLANES, fc), F32), pltpu.VMEM((tm, d_ff), BF16)]
    if sample:
        in_specs += [pl.BlockSpec((tm, d_ff), row), pl.BlockSpec((tm, d_ff), row)]
        args += [s1, s2]
        out_shape = [jax.ShapeDtypeStruct((m, d), F32), jax.ShapeDtypeStruct((m, d_ff), F32)]
        out_specs = [pl.BlockSpec((tm, d), row), pl.BlockSpec((tm, d_ff), row)]
    else:
        nb = nt // tiles_per_batch
        out_shape = [jax.ShapeDtypeStruct((m, d), F32), jax.ShapeDtypeStruct((nb, SUBLANES, d_ff), F32)]
        out_specs = [pl.BlockSpec((tm, d), row),
                     pl.BlockSpec((1, SUBLANES, d_ff), lambda i: (i // tiles_per_batch, 0, 0))]
        scratch.append(pltpu.VMEM((d_ff // fc, SUBLANES, fc), F32))
    kern = functools.partial(_post_kernel, sample=sample, d_ff=d_ff, fc=fc,
                             tiles_per_batch=tiles_per_batch, dec_seq=dec_seq)
    return pl.pallas_call(
        kern, grid=(nt,), in_specs=in_specs, out_specs=out_specs, out_shape=out_shape,
        scratch_shapes=scratch, compiler_params=_cparams(1), name="post_s" if sample else "post_p",
    )(*args)


def _rope_cos_sin(pos, theta, rot):
    half = rot // 2
    inv = jnp.power(theta, -(jnp.arange(half, dtype=F32) * 2.0 / rot))
    ang = pos.astype(F32)[:, None] * inv[None, :]
    return jnp.cos(ang), jnp.sin(ang)


def _tables(pos, heads, rope_d, dh, qk_w):
    cos, sin = _rope_cos_sin(pos, MLA_THETA, rope_d)
    mla_cs = jnp.tile(jnp.concatenate([cos, cos], axis=1), (1, heads))
    mla_sn = jnp.tile(jnp.concatenate([-sin, sin], axis=1), (1, heads))
    rot = dh // 4
    cos, sin = _rope_cos_sin(pos, DIFF_THETA, rot)
    n = pos.shape[0]
    cs = jnp.concatenate([cos, cos, jnp.ones((n, dh - rot), F32)], axis=1)
    sn = jnp.concatenate([-sin, sin, jnp.zeros((n, dh - rot), F32)], axis=1)
    return {"mla_cs": mla_cs, "mla_sn": mla_sn,
            "diff_cs": jnp.tile(cs, (1, qk_w // dh)), "diff_sn": jnp.tile(sn, (1, qk_w // dh))}


def _mix_matrices(w_s, b_s, mb, t):
    mask = jnp.arange(t)[:, None] >= jnp.arange(t)[None, :]
    wt = jnp.where(mask[None], w_s[:, :t, :t], 0.0)
    reps = mb // t
    eye = jnp.eye(reps, dtype=F32)
    wm = jnp.einsum("ab,gts->gatbs", eye, wt).reshape(w_s.shape[0], mb, mb)
    bm = jnp.broadcast_to(jnp.tile(b_s[:, :t], (1, reps))[:, :, None], (w_s.shape[0], mb, LANES))
    return wm.astype(BF16), bm.astype(F32)


def _prep_even(e, p, mb_s, t_s):
    w_in = p["w_in_even"][e]
    g_v = p["gmlp_g_v"][e]
    w_s, b_s = p["gmlp_w_s"][e], p["gmlp_b_s"][e]
    w_uq, w_ukv = p["mla_w_uq"][e], p["mla_w_ukv"][e]
    q_rank, heads, hd = w_uq.shape
    kv_rank = w_ukv.shape[0]
    rope_d = p["cache_mla_rope"].shape[-1]
    nope = hd - rope_d
    vdim = w_ukv.shape[2] - nope
    a_w = g_v.shape[0]
    groups = w_s.shape[0]
    assert a_w // groups == LANES and 2 * rope_d == LANES and heads % 2 == 0
    o = 2 * a_w + q_rank + kv_rank
    kpe = w_in[:, o:o + rope_d]
    half = rope_d // 2
    kpe_sw = jnp.concatenate([kpe[:, half:], kpe[:, :half]], axis=1)
    w_in2 = jnp.concatenate([w_in[:, :o], kpe, kpe, kpe_sw, kpe_sw], axis=1).astype(BF16)
    qr = w_uq[:, :, nope:]
    qr_sw = jnp.concatenate([qr[:, :, half:], qr[:, :, :half]], axis=2)
    w_uq2 = jnp.concatenate([w_uq[:, :, :nope].reshape(q_rank, heads * nope),
                             qr.reshape(q_rank, heads * rope_d),
                             qr_sw.reshape(q_rank, heads * rope_d)], axis=1).astype(BF16)
    w_uk = w_ukv[:, :, :nope]
    w_uv = w_ukv[:, :, nope:]
    w_ukv2 = jnp.concatenate([w_uk.reshape(kv_rank, heads * nope),
                              w_uv.reshape(kv_rank, heads * vdim)], axis=1).astype(BF16)
    hid = jnp.concatenate([jnp.repeat(jnp.arange(heads), nope), jnp.repeat(jnp.arange(heads), rope_d)])
    e_q = (hid[:, None] == hid[None, :]).astype(BF16)
    g_qn, g_kn = p["mla_g_qn"][e], p["mla_g_kn"][e]
    wmix_p, bmix_p = _mix_matrices(w_s, b_s, GMLP_CHUNK, GMLP_CHUNK)
    wmix_s, bmix_s = _mix_matrices(w_s, b_s, mb_s, t_s)
    return {
        "dims": dict(groups=groups, heads=heads, nope=nope, rope_d=rope_d, vdim=vdim, a_w=a_w,
                     q_rank=q_rank, kv_rank=kv_rank),
        "w_in": w_in2, "g_v": g_v[None, :], "wmix_p": wmix_p, "bmix_p": bmix_p,
        "wmix_s": wmix_s, "bmix_s": bmix_s,
        "g_cq": p["mla_g_cq"][e][None, :], "w_uq": w_uq2, "g_ckv": p["mla_g_ckv"][e][None, :],
        "w_ukv": w_ukv2,
        "g_qn": jnp.concatenate([jnp.tile(g_qn[:nope], heads), jnp.tile(g_qn[nope:], heads)])[None, :],
        "g_kn_n": jnp.tile(g_kn[:nope], heads)[None, :], "g_kn_r": jnp.tile(g_kn[nope:], heads)[None, :],
        "e_q": e_q, "e_k": e_q[:heads * nope, :],
        "w_uk_t": jnp.transpose(w_uk, (1, 2, 0)).reshape(heads * nope, kv_rank).astype(BF16),
        "w_uv": jnp.transpose(w_uv, (1, 0, 2)).astype(BF16),
        "w_out": p["w_out_even"][e].astype(BF16),
    }


def _prep_odd(o, p):
    w_in = p["w_in_odd"][o]
    c_w = p["conv_w"].shape[2]
    _, _, _, heads, _, dh = p["cache_diff_k"].shape
    vdim = p["cache_diff_v"].shape[-1]
    qk_w = heads * 2 * dh
    assert 2 * dh == LANES and vdim == LANES
    layer = 2 * o + 1
    return {
        "dims": dict(heads=heads, dh=dh, vdim=vdim, c_w=c_w, qk_w=qk_w, v_w=heads * vdim),
        "w_in": w_in.astype(BF16),
        "g_q": jnp.tile(p["diff_g_q"][o], qk_w // dh)[None, :],
        "g_k": jnp.tile(p["diff_g_k"][o], qk_w // dh)[None, :],
        "conv_w": p["conv_w"][o], "conv_b": p["conv_b"][o][None, :],
        "ln_g": p["conv_ln_g"][o][None, :], "ln_b": p["conv_ln_b"][o][None, :],
        "lq1": p["diff_lq1"][o][None, :], "lk1": p["diff_lk1"][o][None, :],
        "lq2": p["diff_lq2"][o][None, :], "lk2": p["diff_lk2"][o][None, :],
        "g_sub": p["diff_g_sub"][o][None, :],
        "lam_init": 0.8 - 0.6 * math.exp(-0.3 * layer),
        "w_out": p["w_out_odd"][o].astype(BF16),
    }


def _prep_ffn(l, p):
    return {"g_ffn": p["g_ffn"][l][None, :], "w_up": p["ffn_w_up"][l].astype(BF16),
            "conv_w": p["ffn_conv_w"][l], "conv_b": p["ffn_conv_b"][l][None, :],
            "w_down": p["ffn_w_down"][l].astype(BF16)}


def _pad_rows(a, rows):
    return jnp.pad(a, ((0, 0), (0, rows - a.shape[1]), (0, 0)))


def kernel(x_prompt, x_sample, cache_mla_latent, cache_mla_rope, cache_diff_k, cache_diff_v, state_conv, state_ffn, page_table, g_mix, g_ffn, w_in_even, w_out_even, gmlp_g_v, gmlp_w_s, gmlp_b_s, mla_g_cq, mla_w_uq, mla_g_ckv, mla_w_ukv, mla_g_qn, mla_g_kn, w_in_odd, w_out_odd, conv_w, conv_b, conv_ln_g, conv_ln_b, diff_g_q, diff_g_k, diff_lq1, diff_lk1, diff_lq2, diff_lk2, diff_g_sub, ffn_w_up, ffn_conv_w, ffn_conv_b, ffn_w_down):
    p = dict(cache_mla_rope=cache_mla_rope, cache_diff_k=cache_diff_k, cache_diff_v=cache_diff_v,
             g_ffn=g_ffn, w_in_even=w_in_even, w_out_even=w_out_even, gmlp_g_v=gmlp_g_v,
             gmlp_w_s=gmlp_w_s, gmlp_b_s=gmlp_b_s, mla_g_cq=mla_g_cq, mla_w_uq=mla_w_uq,
             mla_g_ckv=mla_g_ckv, mla_w_ukv=mla_w_ukv, mla_g_qn=mla_g_qn, mla_g_kn=mla_g_kn,
             w_in_odd=w_in_odd, w_out_odd=w_out_odd, conv_w=conv_w, conv_b=conv_b,
             conv_ln_g=conv_ln_g, conv_ln_b=conv_ln_b, diff_g_q=diff_g_q, diff_g_k=diff_g_k,
             diff_lq1=diff_lq1, diff_lk1=diff_lk1, diff_lq2=diff_lq2, diff_lk2=diff_lk2,
             diff_g_sub=diff_g_sub, ffn_w_up=ffn_w_up, ffn_conv_w=ffn_conv_w, ffn_conv_b=ffn_conv_b,
             ffn_w_down=ffn_w_down)
    batch, seq, d = x_prompt.shape
    ns, t_s, _ = x_sample.shape
    depth = g_mix.shape[0]
    n_pages, slots = page_table.shape[1], cache_mla_latent.shape[2]
    past = n_pages * slots
    mp, ms = batch * seq, ns * t_s
    tm_p = min(TOKEN_TILE, seq)
    tm_s = ms
    mb_s = min(GMLP_CHUNK, ms)
    q_rows = SUBLANES
    n_conv_state = state_conv.shape[2]
    assert seq % tm_p == 0 and tm_p % GMLP_CHUNK == 0 and ms % mb_s == 0 and mb_s % t_s == 0
    assert 2 <= t_s <= q_rows and t_s <= slots and t_s <= n_conv_state and seq >= 32 and past % GMLP_CHUNK == 0
    tpb = seq // tm_p

    heads_d, dh = cache_diff_k.shape[3], cache_diff_k.shape[5]
    qk_w = heads_d * 2 * dh
    heads_m, rope_d = mla_w_uq.shape[2], cache_mla_rope.shape[-1]
    tabs_p = _tables(jnp.arange(seq, dtype=jnp.int32), heads_m, rope_d, dh, qk_w)
    tabs_s = _tables(jnp.tile(past + jnp.arange(t_s, dtype=jnp.int32), ns), heads_m, rope_d, dh, qk_w)

    cache_rope_t = jnp.transpose(cache_mla_rope, (0, 1, 3, 2))
    n_pool = cache_diff_k.shape[1]
    cache_kt = jnp.transpose(cache_diff_k, (0, 1, 3, 4, 5, 2)).reshape(-1, n_pool, qk_w, slots)
    cache_v = cache_diff_v.reshape(-1, n_pool, slots * heads_d, cache_diff_v.shape[-1])
    pt = page_table.reshape(-1)

    xp = x_prompt.reshape(mp, d)
    xs = x_sample.reshape(ms, d)
    outs = {k: [] for k in ("gv_s", "lat_p", "rope_p", "lat_s", "rope_s", "conv_p", "conv_s",
                            "dk_p", "dv_p", "dk_s", "dv_s", "ffn_p", "ffn_s")}
    for l in range(depth):
        fw = _prep_ffn(l, p)
        if l % 2 == 0:
            e = l // 2
            w = _prep_even(e, p, mb_s, t_s)
            w["g_mix"] = g_mix[l][None, :]
            dims = w["dims"]
            oa_p, lat_p, rope_p, q_p, k_p, v_p = _even_pre(xp, tabs_p, w, sample=False, tm=tm_p, pos_rows=seq)
            ob_p = _mla_flash(q_p, k_p, v_p, batch=batch, seq=seq, vdim=dims["vdim"],
                              scale=(dims["nope"] + dims["rope_d"]) ** -0.5)
            oa_s, gv_s, lat_s, rope_s, q_s = _even_pre(xs, tabs_s, w, sample=True, tm=tm_s, pos_rows=ms)
            q8 = _pad_rows(q_s.reshape(ns, t_s, -1), q_rows)
            c_new = _pad_rows(lat_s.reshape(ns, t_s, -1), slots)
            rt_new = jnp.transpose(_pad_rows(rope_s.reshape(ns, t_s, -1), slots), (0, 2, 1))
            o8 = _mla_paged(pt, q8, c_new, rt_new, cache_mla_latent, cache_rope_t, w, layer=e)
            ob_s = o8[:, :t_s].reshape(ms, -1).astype(BF16)
            outs["gv_s"].append(gv_s.reshape(ns, t_s, -1))
            outs["lat_p"].append(lat_p.reshape(batch, seq, -1))
            outs["rope_p"].append(rope_p.reshape(batch, seq, -1))
            outs["lat_s"].append(lat_s.reshape(ns, t_s, -1))
            outs["rope_s"].append(rope_s.reshape(ns, t_s, -1))
        else:
            o = l // 2
            w = _prep_odd(o, p)
            w["g_mix"] = g_mix[l][None, :]
            dims = w["dims"]
            oa_p, q_p, kf_p, kb_p, vf_p, vb_p, st_p = _odd_pre(xp, tabs_p, w, sample=False, tm=tm_p,
                                                               pos_rows=seq, tiles_per_batch=tpb)
            ob_p = _diff_flash(q_p, kb_p, vb_p, w, batch=batch, seq=seq)
            xc_s, q_s, k_s, v_s = _odd_pre(xs, tabs_s, w, sample=True, tm=tm_s, pos_rows=ms, tiles_per_batch=1)
            state_t = jnp.transpose(state_conv[o], (1, 0, 2))
            xc_t = jnp.transpose(xc_s.reshape(ns, t_s, -1), (1, 0, 2))
            oa_s = jnp.transpose(_conv_sample(state_t, xc_t, w), (1, 0, 2)).reshape(ms, -1)
            q8 = _pad_rows(q_s.reshape(ns, t_s, -1), q_rows)
            kt_new = jnp.transpose(_pad_rows(k_s.reshape(ns, t_s, -1), slots), (0, 2, 1))
            v_new = _pad_rows(v_s.reshape(ns, t_s * heads_d, -1), slots * heads_d)
            o8 = _diff_paged(pt, q8, kt_new, v_new, cache_kt, cache_v, w, layer=o)
            ob_s = o8[:, :t_s].reshape(ms, -1).astype(BF16)
            outs["conv_p"].append(st_p[:, st_p.shape[1] - n_conv_state:, :])
            outs["conv_s"].append(jnp.concatenate([state_conv[o][:, t_s:], xc_s.reshape(ns, t_s, -1)], axis=1))
            outs["dk_p"].append(kf_p.reshape(batch, seq, heads_d, 2, dh))
            outs["dv_p"].append(vf_p.reshape(batch, seq, heads_d, -1))
            outs["dk_s"].append(k_s.reshape(ns, t_s, heads_d, 2, dh))
            outs["dv_s"].append(v_s.reshape(ns, t_s, heads_d, -1))
        xp, ffn_st_p = _post(xp, oa_p, ob_p, w["w_out"], fw, sample=False, tm=tm_p, tiles_per_batch=tpb)
        st = state_ffn[l]
        zero = jnp.zeros((ns, t_s - 1, st.shape[-1]), F32)
        s1 = jnp.concatenate([st[:, 1:2], zero], axis=1).reshape(ms, -1)
        s2 = jnp.concatenate([st, zero[:, 1:]], axis=1).reshape(ms, -1)
        xs, g_s = _post(xs, oa_s, ob_s, w["w_out"], fw, sample=True, tm=tm_s, tiles_per_batch=1,
                        s1=s1, s2=s2, dec_seq=t_s)
        outs["ffn_p"].append(ffn_st_p[:, SUBLANES - 2:, :])
        outs["ffn_s"].append(g_s.reshape(ns, t_s, -1)[:, t_s - 2:, :])
    stk = lambda k: jnp.stack(outs[k])
    return (xp.reshape(batch, seq, d), xs.reshape(ns, t_s, d), stk("gv_s"), stk("lat_p"), stk("rope_p"),
            stk("lat_s"), stk("rope_s"), stk("conv_p"), stk("conv_s"), stk("dk_p"), stk("dv_p"),
            stk("dk_s"), stk("dv_s"), stk("ffn_p"), stk("ffn_s"))
```
